```python
import jax
import jax.numpy as jnp
from jax import lax
import numpy as np

D_MODEL = 1024
BATCH = 2
SEQ = 16384
DEPTH = 2
DEC_BATCH = 32
DEC_SEQ = 64
PAST_LEN = 4096

CHUNK = 64
N_A = DEPTH // 2
N_B = DEPTH - N_A
RET_HEADS = 4
RET_DK = D_MODEL // RET_HEADS
RET_DV = 2 * D_MODEL // RET_HEADS
RET_QK = RET_HEADS * RET_DK
RET_V = RET_HEADS * RET_DV
RET_IN = 2 * RET_QK + 2 * RET_V
MLA_HEADS = 16
QK_NOPE = 64
QK_ROPE = 32
V_HEAD = 64
Q_LORA = 384
KV_LORA = 256
Q_BLOCK = 128
D_FF = ((8 * D_MODEL // 3 + 255) // 256) * 256
ROPE_THETA = 10000.0
NORM_EPS = 1e-6
MLA_SCALE = (QK_NOPE + QK_ROPE) ** -0.5
NEG_INF = -1e30

kernel_name = 'yoco_retention_mla_stream_step'


def rmsnorm(x, g):
    xf = x.astype(jnp.float32)
    y = xf * lax.rsqrt(jnp.mean(xf * xf, axis=-1, keepdims=True) + NORM_EPS)
    return (y * g.astype(jnp.float32)).astype(x.dtype)


def rope(x, pos):
    half = x.shape[-1] // 2
    inv_freq = ROPE_THETA ** (-jnp.arange(half, dtype=jnp.float32) / half)
    ang = pos.astype(jnp.float32)[:, None] * inv_freq[None, :]
    cos = jnp.cos(ang)[:, None, :]
    sin = jnp.sin(ang)[:, None, :]
    xf = x.astype(jnp.float32)
    x1, x2 = xf[..., :half], xf[..., half:]
    return jnp.concatenate([x1 * cos - x2 * sin, x2 * cos + x1 * sin], axis=-1).astype(x.dtype)


def swiglu(h, w_in, w_out):
    gate, up = jnp.split(h @ w_in, 2, axis=-1)
    return (jax.nn.silu(gate) * up) @ w_out


def retention_layer(h, w_in, w_out, state0, pos, chunk):
    b, t, _ = h.shape
    nc = t // chunk
    q, k, v, g = jnp.split(h @ w_in, [RET_QK, 2 * RET_QK, 2 * RET_QK + RET_V], axis=-1)
    q = rope(q.reshape(b, t, RET_HEADS, RET_DK), pos).astype(jnp.float32)
    k = rope(k.reshape(b, t, RET_HEADS, RET_DK), pos).astype(jnp.float32) * (RET_DK ** -0.5)
    v = v.reshape(b, t, RET_HEADS, RET_DV).astype(jnp.float32)
    lg = jnp.log(1.0 - 2.0 ** (-5.0 - jnp.arange(RET_HEADS, dtype=jnp.float32)))
    idx = jnp.arange(chunk, dtype=jnp.float32)
    diff = idx[:, None] - idx[None, :]
    dmask = jnp.where(diff >= 0, jnp.exp(jnp.maximum(diff, 0.0)[None] * lg[:, None, None]), 0.0)
    q_dec = jnp.exp((idx + 1.0)[:, None] * lg[None, :])[:, :, None]
    k_dec = jnp.exp((chunk - 1.0 - idx)[:, None] * lg[None, :])[:, :, None]
    c_dec = jnp.exp(chunk * lg)[:, None, None]

    def to_chunks(a):
        return a.reshape(b, nc, chunk, RET_HEADS, a.shape[-1]).swapaxes(0, 1)

    def step(s, inp):
        qc, kc, vc = inp
        a = jnp.einsum('bqhd,bkhd->bhqk', qc, kc) * dmask
        y = jnp.einsum('bhqk,bkhe->bqhe', a, vc) + jnp.einsum('bqhd,bhde->bqhe', qc * q_dec, s)
        s = s * c_dec + jnp.einsum('bkhd,bkhe->bhde', kc * k_dec, vc)
        return s, y

    s_fin, y = lax.scan(step, state0.astype(jnp.float32), (to_chunks(q), to_chunks(k), to_chunks(v)))
    y = y.swapaxes(0, 1).reshape(b, t, RET_HEADS, RET_DV)
    y = y * lax.rsqrt(jnp.mean(y * y, axis=-1, keepdims=True) + NORM_EPS)
    y = y.reshape(b, t, RET_V).astype(h.dtype)
    return (jax.nn.silu(g) * y) @ w_out, s_fin


def mla_shared_kv(x, pos, g_kv_in, w_dkv, g_ckv):
    h = rmsnorm(x, g_kv_in)
    ckr = h @ w_dkv
    c_kv = rmsnorm(ckr[..., :KV_LORA], g_ckv)
    k_rope = rope(ckr[..., KV_LORA:][:, :, None, :], pos)[:, :, 0, :]
    return c_kv, k_rope


def mla_attend(q_lat, q_rope, c_kv, k_rope, w_uv, mask):
    s = (jnp.einsum('bqhc,bkc->bhqk', q_lat, c_kv)
         + jnp.einsum('bqhr,bkr->bhqk', q_rope, k_rope)).astype(jnp.float32) * MLA_SCALE
    if mask is not None:
        s = jnp.where(mask, s, NEG_INF)
    p = jax.nn.softmax(s, axis=-1).astype(c_kv.dtype)
    o_lat = jnp.einsum('bhqk,bkc->bqhc', p, c_kv)
    return jnp.einsum('bqhc,chv->bqhv', o_lat, w_uv)


def mla_attend_prompt(q_lat, q_rope, c_kv, k_rope, w_uv):
    b, s_len = q_lat.shape[:2]
    nb = s_len // Q_BLOCK
    ql = q_lat.reshape(b, nb, Q_BLOCK, MLA_HEADS, KV_LORA).swapaxes(0, 1)
    qr = q_rope.reshape(b, nb, Q_BLOCK, MLA_HEADS, QK_ROPE).swapaxes(0, 1)
    key_chunk = jnp.arange(s_len, dtype=jnp.int32) // CHUNK

    def block(args):
        ql_b, qr_b, bi = args
        q_chunk = (bi * Q_BLOCK + jnp.arange(Q_BLOCK, dtype=jnp.int32)) // CHUNK
        mask = key_chunk[None, :] <= q_chunk[:, None]
        return mla_attend(ql_b, qr_b, c_kv, k_rope, w_uv, mask)

    out = lax.map(block, (ql, qr, jnp.arange(nb, dtype=jnp.int32)))
    return out.swapaxes(0, 1).reshape(b, s_len, MLA_HEADS, V_HEAD)


def mla_layer(h, pos, c_keys, kr_keys, w_dq, g_q, w_uq, w_uk, w_uv, w_o, is_prompt):
    b, t, _ = h.shape
    c_q = rmsnorm(h @ w_dq, g_q)
    q = (c_q @ w_uq).reshape(b, t, MLA_HEADS, QK_NOPE + QK_ROPE)
    q_rope = rope(q[..., QK_NOPE:], pos)
    q_lat = jnp.einsum('bthn,chn->bthc', q[..., :QK_NOPE], w_uk)
    if is_prompt:
        o = mla_attend_prompt(q_lat, q_rope, c_keys, kr_keys, w_uv)
    else:
        o = mla_attend(q_lat, q_rope, c_keys, kr_keys, w_uv, None)
    return o.reshape(b, t, MLA_HEADS * V_HEAD) @ w_o


def run_trunk(x, pos, ret_state0, past_ckv, past_krope, is_prompt,
              g_mix, g_ffn, w_ret_in, w_ret_out, g_kv_in, w_dkv, g_ckv, w_uk, w_uv,
              w_dq, g_q, w_uq, w_mla_out, w_ffn_in, w_ffn_out, g_final):
    chunk = CHUNK if is_prompt else x.shape[1]
    ret_states = []
    c_kv = k_rope = c_keys = kr_keys = None
    for l in range(DEPTH):
        if l < N_A:
            out, s_new = retention_layer(rmsnorm(x, g_mix[l]), w_ret_in[l], w_ret_out[l],
                                         ret_state0[l], pos, chunk)
            x = x + out
            ret_states.append(s_new)
        else:
            if l == N_A:
                c_kv, k_rope = mla_shared_kv(x, pos, g_kv_in, w_dkv, g_ckv)
                if is_prompt:
                    c_keys, kr_keys = c_kv, k_rope
                else:
                    c_keys = jnp.concatenate([past_ckv.astype(c_kv.dtype), c_kv], axis=1)
                    kr_keys = jnp.concatenate([past_krope.astype(k_rope.dtype), k_rope], axis=1)
            j = l - N_A
            x = x + mla_layer(rmsnorm(x, g_mix[l]), pos, c_keys, kr_keys, w_dq[j], g_q[j], w_uq[j],
                              w_uk, w_uv, w_mla_out[j], is_prompt)
        x = x + swiglu(rmsnorm(x, g_ffn[l]), w_ffn_in[l], w_ffn_out[l])
    return rmsnorm(x, g_final), jnp.stack(ret_states), c_kv, k_rope


def setup_inputs(seed: int = 0) -> dict:
    key = jax.random.key(seed)
    ks = jax.random.split(key, 24)
    f32 = jnp.float32

    def w(k, shape, fan_in):
        return jax.random.normal(k, shape, f32) * (fan_in ** -0.5)

    def gain(k, shape):
        return 1.0 + 0.1 * jax.random.normal(k, shape, f32)

    return {
        'x_prompt': jax.random.normal(ks[0], (BATCH, SEQ, D_MODEL), f32),
        'x_sample': jax.random.normal(ks[1], (DEC_BATCH, DEC_SEQ, D_MODEL), f32),
        'state_ret': 0.5 * jax.random.normal(ks[2], (N_A, DEC_BATCH, RET_HEADS, RET_DK, RET_DV), f32),
        'cache_ckv': jax.random.normal(ks[3], (DEC_BATCH, PAST_LEN, KV_LORA), f32),
        'cache_krope': jax.random.normal(ks[4], (DEC_BATCH, PAST_LEN, QK_ROPE), f32),
        'g_mix': gain(ks[5], (DEPTH, D_MODEL)),
        'g_ffn': gain(ks[6], (DEPTH, D_MODEL)),
        'w_ret_in': w(ks[7], (N_A, D_MODEL, RET_IN), D_MODEL),
        'w_ret_out': w(ks[8], (N_A, RET_V, D_MODEL), RET_V),
        'g_kv_in': gain(ks[9], (D_MODEL,)),
        'w_dkv': w(ks[10], (D_MODEL, KV_LORA + QK_ROPE), D_MODEL),
        'g_ckv': gain(ks[11], (KV_LORA,)),
        'w_uk': w(ks[12], (KV_LORA, MLA_HEADS, QK_NOPE), KV_LORA),
        'w_uv': w(ks[13], (KV_LORA, MLA_HEADS, V_HEAD), KV_LORA),
        'w_dq': w(ks[14], (N_B, D_MODEL, Q_LORA), D_MODEL),
        'g_q': gain(ks[15], (N_B, Q_LORA)),
        'w_uq': w(ks[16], (N_B, Q_LORA, MLA_HEADS * (QK_NOPE + QK_ROPE)), Q_LORA),
        'w_mla_out': w(ks[17], (N_B, MLA_HEADS * V_HEAD, D_MODEL), MLA_HEADS * V_HEAD),
        'w_ffn_in': w(ks[18], (DEPTH, D_MODEL, 2 * D_FF), D_MODEL),
        'w_ffn_out': w(ks[19], (DEPTH, D_FF, D_MODEL), D_FF),
        'g_final': gain(ks[20], (D_MODEL,)),
    }


def reference(x_prompt, x_sample, state_ret, cache_ckv, cache_krope, g_mix, g_ffn, w_ret_in, w_ret_out,
              g_kv_in, w_dkv, g_ckv, w_uk, w_uv, w_dq, g_q, w_uq, w_mla_out, w_ffn_in, w_ffn_out, g_final):
    past = cache_ckv.shape[1]
    pos_p = jnp.arange(x_prompt.shape[1], dtype=jnp.int32)
    pos_s = past + jnp.arange(x_sample.shape[1], dtype=jnp.int32)
    zero_state = jnp.zeros((N_A, x_prompt.shape[0], RET_HEADS, RET_DK, RET_DV), jnp.float32)
    y_prompt, st_p, ckv_p, kr_p = run_trunk(
        x_prompt, pos_p, zero_state, None, None, True,
        g_mix, g_ffn, w_ret_in, w_ret_out, g_kv_in, w_dkv, g_ckv, w_uk, w_uv,
        w_dq, g_q, w_uq, w_mla_out, w_ffn_in, w_ffn_out, g_final)
    y_sample, st_s, ckv_s, kr_s = run_trunk(
        x_sample, pos_s, state_ret, cache_ckv, cache_krope, False,
        g_mix, g_ffn, w_ret_in, w_ret_out, g_kv_in, w_dkv, g_ckv, w_uk, w_uv,
        w_dq, g_q, w_uq, w_mla_out, w_ffn_in, w_ffn_out, g_final)
    return (y_prompt, y_sample, st_p.astype(state_ret.dtype), st_s.astype(state_ret.dtype),
            ckv_p, kr_p, ckv_s, kr_s)
```

```python
import functools

import jax
import jax.numpy as jnp
from jax import lax
from jax.experimental import pallas as pl
from jax.experimental.pallas import tpu as pltpu

F32 = jnp.float32
BF16 = jnp.bfloat16

D_MODEL = 1024
CHUNK = 64
RET_HEADS = 4
RET_DK = D_MODEL // RET_HEADS
RET_DV = 2 * D_MODEL // RET_HEADS
RET_QK = RET_HEADS * RET_DK
RET_V = RET_HEADS * RET_DV
MLA_HEADS = 16
QK_NOPE = 64
QK_ROPE = 32
V_HEAD = 64
Q_LORA = 384
KV_LORA = 256
D_FF = 2816
ROPE_THETA = 10000.0
NORM_EPS = 1e-6
MLA_SCALE = (QK_NOPE + QK_ROPE) ** -0.5
NEG_INF = -1e30

V7X_VMEM_BYTES = 64 * 2**20
VMEM_LIMIT_BYTES = V7X_VMEM_BYTES - 8 * 2**20
LANES = 128
MXU_DIM = 256

RET_CHUNK = MXU_DIM
KV_BLOCK = MXU_DIM
TM_RET_IN = 512
TM_FFN = 256
TM_KVQ = 256

_NT = (((1,), (1,)), ((), ()))
_TN = (((0,), (0,)), ((), ()))


def _cparams(*sem):
    return pltpu.CompilerParams(dimension_semantics=sem, vmem_limit_bytes=VMEM_LIMIT_BYTES)


def _resident(shape):
    nd = len(shape)
    return pl.BlockSpec(shape, lambda *_: (0,) * nd, pipeline_mode=pl.Buffered(1))


def _rms(x, g):
    return x * lax.rsqrt(jnp.mean(x * x, axis=-1, keepdims=True) + NORM_EPS) * g


def _dot(a, b):
    return jnp.dot(a, b, preferred_element_type=F32)


def _silu(x):
    return x * jax.nn.sigmoid(x)


def _ret_in_kernel(x_ref, g_ref, w_ref, cos_ref, sin_ref, q_ref, k_ref, v_ref, sg_ref):
    hb = _rms(x_ref[...], g_ref[...]).astype(BF16)
    cos = cos_ref[...]
    sin = sin_ref[...]
    half = RET_DK // 2
    for dst, base, scale in ((q_ref, 0, 1.0), (k_ref, RET_QK, RET_DK ** -0.5)):
        for hd in range(RET_HEADS):
            c0 = hd * RET_DK
            y = _dot(hb, w_ref[:, base + c0:base + c0 + RET_DK])
            y1 = y[:, :half]
            y2 = y[:, half:]
            dst[:, c0:c0 + half] = ((y1 * cos - y2 * sin) * scale).astype(BF16)
            dst[:, c0 + half:c0 + RET_DK] = ((y2 * cos + y1 * sin) * scale).astype(BF16)
    v_ref[...] = _dot(hb, w_ref[:, 2 * RET_QK:2 * RET_QK + RET_V]).astype(BF16)
    sg_ref[...] = _silu(_dot(hb, w_ref[:, 2 * RET_QK + RET_V:])).astype(BF16)


def _ret_in(x, g, w, cos, sin, tm):
    n = x.shape[0]
    nt = cos.shape[0] // tm
    tok = lambda d: pl.BlockSpec((tm, d), lambda i: (i, 0))
    tab = pl.BlockSpec((tm, RET_DK // 2), lambda i: (i % nt, 0))
    return pl.pallas_call(
        _ret_in_kernel,
        grid=(n // tm,),
        in_specs=[tok(D_MODEL), _resident(g.shape), _resident(w.shape), tab, tab],
        out_specs=[tok(RET_QK), tok(RET_QK), tok(RET_V), tok(RET_V)],
        out_shape=[jax.ShapeDtypeStruct((n, RET_QK), BF16), jax.ShapeDtypeStruct((n, RET_QK), BF16),
                   jax.ShapeDtypeStruct((n, RET_V), BF16), jax.ShapeDtypeStruct((n, RET_V), BF16)],
        compiler_params=_cparams("parallel"),
        name="ret_in",
    )(x, g, w, cos, sin)


def _ret_scan_kernel(q_ref, k_ref, v_ref, sg_ref, dm_ref, qd_ref, kd_ref, cd_ref, s0_ref,
                     z_ref, sout_ref, s_scr):
    c = pl.program_id(2)

    @pl.when(c == 0)
    def _():
        s_scr[...] = s0_ref[...]

    q = q_ref[...]
    k = k_ref[...]
    v = v_ref[...]
    s = s_scr[...]
    a = lax.dot_general(q, k, _NT, preferred_element_type=F32) * dm_ref[...]
    y = _dot(a.astype(BF16), v) + qd_ref[...] * _dot(q, s.astype(BF16))
    kd = (k.astype(F32) * kd_ref[...]).astype(BF16)
    s_new = s * cd_ref[...] + lax.dot_general(kd, v, _TN, preferred_element_type=F32)
    s_scr[...] = s_new
    yn = y * lax.rsqrt(jnp.mean(y * y, axis=-1, keepdims=True) + NORM_EPS)
    z_ref[...] = (sg_ref[...].astype(F32) * yn).astype(BF16)

    @pl.when(c == pl.num_programs(2) - 1)
    def _():
        sout_ref[...] = s_new


def _ret_scan(q, k, v, sg, state0, b, t):
    ck = min(RET_CHUNK, t)
    nc = t // ck
    lg = jnp.log(1.0 - 2.0 ** (-5.0 - jnp.arange(RET_HEADS, dtype=F32)))
    idx = jnp.arange(ck, dtype=F32)
    diff = idx[:, None] - idx[None, :]
    dmask = jnp.where(diff >= 0, jnp.exp(jnp.maximum(diff, 0.0)[None] * lg[:, None, None]), 0.0)
    qdec = jnp.exp((idx + 1.0)[None, :] * lg[:, None])[:, :, None]
    kdec = jnp.exp((ck - 1.0 - idx)[None, :] * lg[:, None])[:, :, None]
    cdec = jnp.broadcast_to(jnp.exp(ck * lg)[:, None, None], (RET_HEADS, 1, RET_DV))

    qk_spec = pl.BlockSpec((None, ck, RET_DK), lambda bi, h, c: (bi, c, h))
    v_spec = pl.BlockSpec((None, ck, RET_DV), lambda bi, h, c: (bi, c, h))
    head = lambda r, w: pl.BlockSpec((None, r, w), lambda bi, h, c: (h, 0, 0))
    st_spec = pl.BlockSpec((None, None, RET_DK, RET_DV), lambda bi, h, c: (bi, h, 0, 0))
    z, s_out = pl.pallas_call(
        _ret_scan_kernel,
        grid=(b, RET_HEADS, nc),
        in_specs=[qk_spec, qk_spec, v_spec, v_spec, head(ck, ck), head(ck, 1), head(ck, 1),
                  head(1, RET_DV), st_spec],
        out_specs=[v_spec, st_spec],
        out_shape=[jax.ShapeDtypeStruct((b, t, RET_V), BF16),
                   jax.ShapeDtypeStruct((b, RET_HEADS, RET_DK, RET_DV), F32)],
        scratch_shapes=[pltpu.VMEM((RET_DK, RET_DV), F32)],
        compiler_params=_cparams("parallel", "parallel", "arbitrary"),
        name="ret_scan",
    )(q.reshape(b, t, RET_QK), k.reshape(b, t, RET_QK), v.reshape(b, t, RET_V), sg.reshape(b, t, RET_V),
      dmask, qdec, kdec, cdec, state0)
    return z.reshape(b * t, RET_V), s_out


def _mix_ffn_kernel(x_ref, z_ref, wm_ref, g_ref, win_ref, wout_ref, gf_ref, o_ref, *, final):
    x1 = x_ref[...] + _dot(z_ref[...], wm_ref[...])
    hb = _rms(x1, g_ref[...]).astype(BF16)
    gu = _dot(hb, win_ref[...])
    a = (_silu(gu[:, :D_FF]) * gu[:, D_FF:]).astype(BF16)
    x2 = x1 + _dot(a, wout_ref[...])
    if final:
        x2 = _rms(x2, gf_ref[...])
    o_ref[...] = x2


def _mix_ffn(x, z, wm, g, win, wout, gf, final, tm):
    n = x.shape[0]
    tok = lambda d: pl.BlockSpec((tm, d), lambda i: (i, 0))
    return pl.pallas_call(
        functools.partial(_mix_ffn_kernel, final=final),
        grid=(n // tm,),
        in_specs=[tok(D_MODEL), tok(z.shape[1]), _resident(wm.shape), _resident(g.shape),
                  _resident(win.shape), _resident(wout.shape), _resident(gf.shape)],
        out_specs=tok(D_MODEL),
        out_shape=jax.ShapeDtypeStruct((n, D_MODEL), F32),
        compiler_params=_cparams("parallel"),
        name="mix_ffn_final" if final else "mix_ffn",
    )(x, z, wm, g, win, wout, gf)


def _kvq_kernel(x_ref, gkv_ref, wdkv_ref, gckv_ref, gmix_ref, wdq_ref, gq_ref, wqn_ref, wqr_ref, wqs_ref,
                ck_ref, sk_ref, cq_ref, sq_ref,
                ckv_ref, kr_ref, ckvb_ref, krb_ref, qn_ref, qr_ref):
    x = x_ref[...]
    xn = x * lax.rsqrt(jnp.mean(x * x, axis=-1, keepdims=True) + NORM_EPS)
    ckr = _dot((xn * gkv_ref[...]).astype(BF16), wdkv_ref[...])
    ckv = _rms(ckr[:, :KV_LORA], gckv_ref[...])
    ckv_ref[...] = ckv
    ckvb_ref[...] = ckv.astype(BF16)
    kr = (ckr[:, KV_LORA:KV_LORA + QK_ROPE] * ck_ref[...]
          + ckr[:, KV_LORA + QK_ROPE:KV_LORA + 2 * QK_ROPE] * sk_ref[...])
    kr_ref[...] = kr
    krb_ref[...] = kr.astype(BF16)
    cq = _rms(_dot((xn * gmix_ref[...]).astype(BF16), wdq_ref[...]), gq_ref[...]).astype(BF16)
    qn_ref[...] = _dot(cq, wqn_ref[...]).astype(BF16)
    qr = _dot(cq, wqr_ref[...])
    qs = _dot(cq, wqs_ref[...])
    cos = cq_ref[...]
    sin = sq_ref[...]
    for j in range(MLA_HEADS * QK_ROPE // LANES):
        sl = slice(j * LANES, (j + 1) * LANES)
        qr_ref[:, sl] = (qr[:, sl] * cos + qs[:, sl] * sin).astype(BF16)


def _kvq(x, gkv, wdkv, gckv, gmix, wdq, gq, wqn, wqr, wqs, ck, sk, cq, sq, tm):
    n = x.shape[0]
    nt = ck.shape[0] // tm
    tok = lambda d: pl.BlockSpec((tm, d), lambda i: (i, 0))
    tab = lambda d: pl.BlockSpec((tm, d), lambda i: (i % nt, 0))
    res = [gkv, wdkv, gckv, gmix, wdq, gq, wqn, wqr, wqs]
    sds = lambda d, dt: jax.ShapeDtypeStruct((n, d), dt)
    return pl.pallas_call(
        _kvq_kernel,
        grid=(n // tm,),
        in_specs=[tok(D_MODEL)] + [_resident(a.shape) for a in res]
                 + [tab(QK_ROPE), tab(QK_ROPE), tab(LANES), tab(LANES)],
        out_specs=[tok(KV_LORA), tok(QK_ROPE), tok(KV_LORA), tok(QK_ROPE),
                   tok(MLA_HEADS * QK_NOPE), tok(MLA_HEADS * QK_ROPE)],
        out_shape=[sds(KV_LORA, F32), sds(QK_ROPE, F32), sds(KV_LORA, BF16), sds(QK_ROPE, BF16),
                   sds(MLA_HEADS * QK_NOPE, BF16), sds(MLA_HEADS * QK_ROPE, BF16)],
        compiler_params=_cparams("parallel"),
        name="kvq",
    )(x, *res, ck, sk, cq, sq)


def _attn_kernel(qn_ref, qr_ref, ckv_ref, krt_ref, wuk_ref, wuv_ref, o_ref,
                 q_s, qr_s, m_s, l_s, acc_s, *, causal, const_limit):
    tq = qn_ref.shape[0]
    kb = ckv_ref.shape[1]
    for g in range(MLA_HEADS // 2):
        ql = _dot(qn_ref[:, g * LANES:(g + 1) * LANES], wuk_ref[g]) * MLA_SCALE
        q_s[(2 * g) * tq:(2 * g + 1) * tq, :] = ql[:, :KV_LORA].astype(BF16)
        q_s[(2 * g + 1) * tq:(2 * g + 2) * tq, :] = ql[:, KV_LORA:].astype(BF16)
    for h in range(MLA_HEADS):
        qr_s[h * tq:(h + 1) * tq, :] = (
            qr_ref[:, h * QK_ROPE:(h + 1) * QK_ROPE].astype(F32) * MLA_SCALE).astype(BF16)
    m_s[...] = jnp.full(m_s.shape, NEG_INF, F32)
    l_s[...] = jnp.zeros(l_s.shape, F32)
    acc_s[...] = jnp.zeros(acc_s.shape, F32)

    lim = (pl.program_id(1) + 1) * tq if causal else const_limit
    nfull = lim // kb
    rem = lim - nfull * kb

    def step(j, masked):
        kblk = ckv_ref[j]
        s = (lax.dot_general(q_s[...], kblk, _NT, preferred_element_type=F32)
             + _dot(qr_s[...], krt_ref[j]))
        if masked:
            col = lax.broadcasted_iota(jnp.int32, s.shape, 1)
            s = jnp.where(col < rem, s, NEG_INF)
        m_prev = m_s[...]
        m_new = jnp.maximum(m_prev, jnp.max(s, axis=1, keepdims=True))
        alpha = jnp.exp(m_prev - m_new)
        p = jnp.exp(s - m_new)
        l_s[...] = alpha * l_s[...] + jnp.sum(p, axis=1, keepdims=True)
        acc_s[...] = alpha * acc_s[...] + _dot(p.astype(BF16), kblk)
        m_s[...] = m_new

    def body(j, carry):
        step(j, False)
        return carry

    lax.fori_loop(0, nfull, body, 0)

    @pl.when(rem > 0)
    def _():
        step(nfull, True)

    o_lat = acc_s[...] / l_s[...]
    for g in range(MLA_HEADS // 2):
        pair = jnp.concatenate([o_lat[(2 * g) * tq:(2 * g + 1) * tq],
                                o_lat[(2 * g + 1) * tq:(2 * g + 2) * tq]], axis=1).astype(BF16)
        o_ref[:, g * LANES:(g + 1) * LANES] = _dot(pair, wuv_ref[g]).astype(BF16)


def _attn(qn, qr, ckv_blocks, krt_blocks, wuk_pairs, wuv_pairs, b, t, causal, const_limit):
    tq = CHUNK
    nq = t // tq
    nkb = ckv_blocks.shape[1]
    rows = MLA_HEADS * tq
    qspec = lambda d: pl.BlockSpec((tq, d), lambda bi, i: (bi * nq + i, 0))
    return pl.pallas_call(
        functools.partial(_attn_kernel, causal=causal, const_limit=const_limit),
        grid=(b, nq),
        in_specs=[qspec(MLA_HEADS * QK_NOPE), qspec(MLA_HEADS * QK_ROPE),
                  pl.BlockSpec((None, nkb, KV_BLOCK, KV_LORA), lambda bi, i: (bi, 0, 0, 0)),
                  pl.BlockSpec((None, nkb, QK_ROPE, KV_BLOCK), lambda bi, i: (bi, 0, 0, 0)),
                  _resident(wuk_pairs.shape), _resident(wuv_pairs.shape)],
        out_specs=qspec(MLA_HEADS * V_HEAD),
        out_shape=jax.ShapeDtypeStruct((b * t, MLA_HEADS * V_HEAD), BF16),
        scratch_shapes=[pltpu.VMEM((rows, KV_LORA), BF16), pltpu.VMEM((rows, QK_ROPE), BF16),
                        pltpu.VMEM((rows, 1), F32), pltpu.VMEM((rows, 1), F32),
                        pltpu.VMEM((rows, KV_LORA), F32)],
        compiler_params=_cparams("parallel", "arbitrary"),
        name="attn_causal" if causal else "attn_full",
    )(qn, qr, ckv_blocks, krt_blocks, wuk_pairs, wuv_pairs)


def _rope_tables(pos, half):
    inv_freq = ROPE_THETA ** (-jnp.arange(half, dtype=F32) / half)
    ang = pos.astype(F32)[:, None] * inv_freq[None, :]
    return jnp.cos(ang), jnp.sin(ang)


def _tile_rows(tab, tm):
    reps = -(-tm // tab.shape[0])
    return jnp.tile(tab, (reps, 1)) if reps > 1 else tab


def _prep_weights(g_mix, g_ffn, w_ret_in, w_ret_out, g_kv_in, w_dkv, g_ckv, w_uk, w_uv,
                  w_dq, g_q, w_uq, w_mla_out, w_ffn_in, w_ffn_out, g_final):
    half = QK_ROPE // 2
    row = lambda g: g.reshape(1, -1)
    w_kr = w_dkv[:, KV_LORA:]
    wdkv = jnp.concatenate([w_dkv, w_kr[:, half:], w_kr[:, :half]], axis=1).astype(BF16)
    wuq3 = w_uq[0].reshape(Q_LORA, MLA_HEADS, QK_NOPE + QK_ROPE)
    wqn = wuq3[:, :, :QK_NOPE].reshape(Q_LORA, -1).astype(BF16)
    wqr = wuq3[:, :, QK_NOPE:].reshape(Q_LORA, -1).astype(BF16)
    wqs = jnp.concatenate([wuq3[:, :, QK_NOPE + half:], wuq3[:, :, QK_NOPE:QK_NOPE + half]],
                          axis=2).reshape(Q_LORA, -1).astype(BF16)
    wuk_t = jnp.transpose(w_uk, (1, 2, 0)).reshape(MLA_HEADS // 2, 2, QK_NOPE, KV_LORA)
    zk = jnp.zeros_like(wuk_t[:, 0])
    wuk_pairs = jnp.concatenate([jnp.concatenate([wuk_t[:, 0], zk], axis=2),
                                 jnp.concatenate([zk, wuk_t[:, 1]], axis=2)], axis=1).astype(BF16)
    wuv_h = jnp.transpose(w_uv, (1, 0, 2)).reshape(MLA_HEADS // 2, 2, KV_LORA, V_HEAD)
    zv = jnp.zeros_like(wuv_h[:, 0])
    wuv_pairs = jnp.concatenate([jnp.concatenate([wuv_h[:, 0], zv], axis=2),
                                 jnp.concatenate([zv, wuv_h[:, 1]], axis=2)], axis=1).astype(BF16)
    return dict(
        g_mix0=row(g_mix[0]), g_mix1=row(g_mix[1]), g_ffn0=row(g_ffn[0]), g_ffn1=row(g_ffn[1]),
        w_ret_in=w_ret_in[0].astype(BF16), w_ret_out=w_ret_out[0].astype(BF16),
        g_kv_in=row(g_kv_in), wdkv=wdkv, g_ckv=row(g_ckv), wuk_pairs=wuk_pairs, wuv_pairs=wuv_pairs,
        w_dq=w_dq[0].astype(BF16), g_q=row(g_q[0]), wqn=wqn, wqr=wqr, wqs=wqs,
        w_mla_out=w_mla_out[0].astype(BF16),
        w_ffn_in0=w_ffn_in[0].astype(BF16), w_ffn_in1=w_ffn_in[1].astype(BF16),
        w_ffn_out0=w_ffn_out[0].astype(BF16), w_ffn_out1=w_ffn_out[1].astype(BF16),
        g_final=row(g_final))


def _trunk(x, pos, state0, past_ckv, past_krope, w):
    b, t, _ = x.shape
    n = b * t
    is_prompt = past_ckv is None
    xf = x.reshape(n, D_MODEL)

    cos_r, sin_r = _rope_tables(pos, RET_DK // 2)
    tm = min(TM_RET_IN, n)
    q, k, v, sg = _ret_in(xf, w["g_mix0"], w["w_ret_in"], _tile_rows(cos_r, tm), _tile_rows(sin_r, tm), tm)
    z, s_fin = _ret_scan(q, k, v, sg, state0, b, t)
    x1 = _mix_ffn(xf, z, w["w_ret_out"], w["g_ffn0"], w["w_ffn_in0"], w["w_ffn_out0"], w["g_final"],
                  False, min(TM_FFN, n))

    cos_m, sin_m = _rope_tables(pos, QK_ROPE // 2)
    ck = jnp.concatenate([cos_m, cos_m], axis=1)
    sk = jnp.concatenate([-sin_m, sin_m], axis=1)
    tm = min(TM_KVQ, n)
    reps = LANES // QK_ROPE
    ckv, kr, ckv_b, kr_b, qn, qr = _kvq(
        x1, w["g_kv_in"], w["wdkv"], w["g_ckv"], w["g_mix1"], w["w_dq"], w["g_q"],
        w["wqn"], w["wqr"], w["wqs"], _tile_rows(ck, tm), _tile_rows(sk, tm),
        _tile_rows(jnp.tile(ck, (1, reps)), tm), _tile_rows(jnp.tile(sk, (1, reps)), tm), tm)

    if is_prompt:
        keys_c = ckv_b.reshape(b, t, KV_LORA)
        keys_r = kr_b.reshape(b, t, QK_ROPE)
        limit = 0
    else:
        keys_c = jnp.concatenate([past_ckv.astype(BF16), ckv_b.reshape(b, t, KV_LORA)], axis=1)
        keys_r = jnp.concatenate([past_krope.astype(BF16), kr_b.reshape(b, t, QK_ROPE)], axis=1)
        limit = keys_c.shape[1]
    pad = (-keys_c.shape[1]) % KV_BLOCK
    if pad:
        keys_c = jnp.pad(keys_c, ((0, 0), (0, pad), (0, 0)))
        keys_r = jnp.pad(keys_r, ((0, 0), (0, pad), (0, 0)))
    nkb = keys_c.shape[1] // KV_BLOCK
    ckv_blocks = keys_c.reshape(b, nkb, KV_BLOCK, KV_LORA)
    krt_blocks = jnp.swapaxes(keys_r.reshape(b, nkb, KV_BLOCK, QK_ROPE), 2, 3)
    o = _attn(qn, qr, ckv_blocks, krt_blocks, w["wuk_pairs"], w["wuv_pairs"], b, t, is_prompt, limit)

    y = _mix_ffn(x1, o, w["w_mla_out"], w["g_ffn1"], w["w_ffn_in1"], w["w_ffn_out1"], w["g_final"],
                 True, min(TM_FFN, n))
    return (y.reshape(b, t, D_MODEL), s_fin[None], ckv.reshape(b, t, KV_LORA), kr.reshape(b, t, QK_ROPE))


def kernel(x_prompt, x_sample, state_ret, cache_ckv, cache_krope, g_mix, g_ffn, w_ret_in, w_ret_out,
           g_kv_in, w_dkv, g_ckv, w_uk, w_uv, w_dq, g_q, w_uq, w_mla_out, w_ffn_in, w_ffn_out, g_final):
    assert g_mix.shape[0] == 2 and w_ret_in.shape[0] == 1 and w_dq.shape[0] == 1
    assert x_prompt.shape[1] % RET_CHUNK == 0 and x_sample.shape[1] == CHUNK
    w = _prep_weights(g_mix, g_ffn, w_ret_in, w_ret_out, g_kv_in, w_dkv, g_ckv, w_uk, w_uv,
                      w_dq, g_q, w_uq, w_mla_out, w_ffn_in, w_ffn_out, g_final)
    past = cache_ckv.shape[1]
    pos_p = jnp.arange(x_prompt.shape[1], dtype=jnp.int32)
    pos_s = past + jnp.arange(x_sample.shape[1], dtype=jnp.int32)
    zero_state = jnp.zeros((x_prompt.shape[0], RET_HEADS, RET_DK, RET_DV), F32)
    y_p, st_p, ckv_p, kr_p = _trunk(x_prompt, pos_p, zero_state, None, None, w)
    y_s, st_s, ckv_s, kr_s = _trunk(x_sample, pos_s, state_ret[0], cache_ckv, cache_krope, w)
    return (y_p, y_s, st_p.astype(state_ret.dtype), st_s.astype(state_ret.dtype), ckv_p, kr_p, ckv_s, kr_s)
```

```python
import functools
import math

import jax
import jax.numpy as jnp
from jax import lax
from jax.experimental import pallas as pl
from jax.experimental.pallas import tpu as pltpu

F32 = jnp.float32
BF16 = jnp.bfloat16

D_MODEL = 1024
CHUNK = 64
RET_HEADS = 4
RET_DK = D_MODEL // RET_HEADS
RET_DV = 2 * D_MODEL // RET_HEADS
RET_QK = RET_HEADS * RET_DK
RET_V = RET_HEADS * RET_DV
MLA_HEADS = 16
QK_NOPE = 64
QK_ROPE = 32
V_HEAD = 64
Q_LORA = 384
KV_LORA = 256
D_FF = 2816
ROPE_THETA = 10000.0
NORM_EPS = 1e-6
MLA_SCALE = (QK_NOPE + QK_ROPE) ** -0.5
NEG_INF = -1e30

V7X_VMEM_BYTES = 64 * 2**20
VMEM_LIMIT_BYTES = V7X_VMEM_BYTES - 8 * 2**20
LANES = 128
MXU_DIM = 256

RET_CHUNK = MXU_DIM
KV_BLOCK = MXU_DIM
ATTN_ROWS = 256
LOG2E = math.log2(math.e)
TM_RET_IN = 512
TM_FFN = 256
TM_KVQ = 256

_NT = (((1,), (1,)), ((), ()))
_TN = (((0,), (0,)), ((), ()))


def _cparams(*sem):
    return pltpu.CompilerParams(dimension_semantics=sem, vmem_limit_bytes=VMEM_LIMIT_BYTES)


def _resident(shape):
    nd = len(shape)
    return pl.BlockSpec(shape, lambda *_: (0,) * nd, pipeline_mode=pl.Buffered(1))


def _rms(x, g):
    return x * lax.rsqrt(jnp.mean(x * x, axis=-1, keepdims=True) + NORM_EPS) * g


def _dot(a, b):
    return jnp.dot(a, b, preferred_element_type=F32)


def _silu(x):
    return x * jax.nn.sigmoid(x)


def _ret_in_kernel(x_ref, g_ref, w_ref, cos_ref, sin_ref, q_ref, k_ref, v_ref, sg_ref):
    hb = _rms(x_ref[...], g_ref[...]).astype(BF16)
    cos = cos_ref[...]
    sin = sin_ref[...]
    half = RET_DK // 2
    for dst, base, scale in ((q_ref, 0, 1.0), (k_ref, RET_QK, RET_DK ** -0.5)):
        for hd in range(RET_HEADS):
            c0 = hd * RET_DK
            y = _dot(hb, w_ref[:, base + c0:base + c0 + RET_DK])
            y1 = y[:, :half]
            y2 = y[:, half:]
            dst[:, c0:c0 + half] = ((y1 * cos - y2 * sin) * scale).astype(BF16)
            dst[:, c0 + half:c0 + RET_DK] = ((y2 * cos + y1 * sin) * scale).astype(BF16)
    v_ref[...] = _dot(hb, w_ref[:, 2 * RET_QK:2 * RET_QK + RET_V]).astype(BF16)
    sg_ref[...] = _silu(_dot(hb, w_ref[:, 2 * RET_QK + RET_V:])).astype(BF16)


def _ret_in(x, g, w, cos, sin, tm):
    n = x.shape[0]
    nt = cos.shape[0] // tm
    tok = lambda d: pl.BlockSpec((tm, d), lambda i: (i, 0))
    tab = pl.BlockSpec((tm, RET_DK // 2), lambda i: (i % nt, 0))
    return pl.pallas_call(
        _ret_in_kernel,
        grid=(n // tm,),
        in_specs=[tok(D_MODEL), _resident(g.shape), _resident(w.shape), tab, tab],
        out_specs=[tok(RET_QK), tok(RET_QK), tok(RET_V), tok(RET_V)],
        out_shape=[jax.ShapeDtypeStruct((n, RET_QK), BF16), jax.ShapeDtypeStruct((n, RET_QK), BF16),
                   jax.ShapeDtypeStruct((n, RET_V), BF16), jax.ShapeDtypeStruct((n, RET_V), BF16)],
        compiler_params=_cparams("parallel"),
        name="ret_in",
    )(x, g, w, cos, sin)


def _ret_scan_kernel(q_ref, k_ref, v_ref, sg_ref, dm_ref, qd_ref, kd_ref, cd_ref, s0_ref,
                     z_ref, sout_ref, s_scr):
    c = pl.program_id(2)

    @pl.when(c == 0)
    def _():
        s_scr[...] = s0_ref[...]

    q = q_ref[...]
    k = k_ref[...]
    v = v_ref[...]
    s = s_scr[...]
    a = lax.dot_general(q, k, _NT, preferred_element_type=F32) * dm_ref[...]
    y = _dot(a.astype(BF16), v) + qd_ref[...] * _dot(q, s.astype(BF16))
    kd = (k.astype(F32) * kd_ref[...]).astype(BF16)
    s_new = s * cd_ref[...] + lax.dot_general(kd, v, _TN, preferred_element_type=F32)
    s_scr[...] = s_new
    yn = y * lax.rsqrt(jnp.mean(y * y, axis=-1, keepdims=True) + NORM_EPS)
    z_ref[...] = (sg_ref[...].astype(F32) * yn).astype(BF16)

    @pl.when(c == pl.num_programs(2) - 1)
    def _():
        sout_ref[...] = s_new


def _ret_scan(q, k, v, sg, state0, b, t):
    ck = min(RET_CHUNK, t)
    nc = t // ck
    lg = jnp.log(1.0 - 2.0 ** (-5.0 - jnp.arange(RET_HEADS, dtype=F32)))
    idx = jnp.arange(ck, dtype=F32)
    diff = idx[:, None] - idx[None, :]
    dmask = jnp.where(diff >= 0, jnp.exp(jnp.maximum(diff, 0.0)[None] * lg[:, None, None]), 0.0)
    qdec = jnp.exp((idx + 1.0)[None, :] * lg[:, None])[:, :, None]
    kdec = jnp.exp((ck - 1.0 - idx)[None, :] * lg[:, None])[:, :, None]
    cdec = jnp.broadcast_to(jnp.exp(ck * lg)[:, None, None], (RET_HEADS, 1, RET_DV))

    qk_spec = pl.BlockSpec((None, ck, RET_DK), lambda bi, h, c: (bi, c, h))
    v_spec = pl.BlockSpec((None, ck, RET_DV), lambda bi, h, c: (bi, c, h))
    head = lambda r, w: pl.BlockSpec((None, r, w), lambda bi, h, c: (h, 0, 0))
    st_spec = pl.BlockSpec((None, None, RET_DK, RET_DV), lambda bi, h, c: (bi, h, 0, 0))
    z, s_out = pl.pallas_call(
        _ret_scan_kernel,
        grid=(b, RET_HEADS, nc),
        in_specs=[qk_spec, qk_spec, v_spec, v_spec, head(ck, ck), head(ck, 1), head(ck, 1),
                  head(1, RET_DV), st_spec],
        out_specs=[v_spec, st_spec],
        out_shape=[jax.ShapeDtypeStruct((b, t, RET_V), BF16),
                   jax.ShapeDtypeStruct((b, RET_HEADS, RET_DK, RET_DV), F32)],
        scratch_shapes=[pltpu.VMEM((RET_DK, RET_DV), F32)],
        compiler_params=_cparams("parallel", "parallel", "arbitrary"),
        name="ret_scan",
    )(q.reshape(b, t, RET_QK), k.reshape(b, t, RET_QK), v.reshape(b, t, RET_V), sg.reshape(b, t, RET_V),
      dmask, qdec, kdec, cdec, state0)
    return z.reshape(b * t, RET_V), s_out


def _mix_ffn_kernel(x_ref, z_ref, wm_ref, g_ref, win_ref, wout_ref, gf_ref, o_ref, *, final):
    x1 = x_ref[...] + _dot(z_ref[...], wm_ref[...])
    hb = _rms(x1, g_ref[...]).astype(BF16)
    gu = _dot(hb, win_ref[...])
    a = (_silu(gu[:, :D_FF]) * gu[:, D_FF:]).astype(BF16)
    x2 = x1 + _dot(a, wout_ref[...])
    if final:
        x2 = _rms(x2, gf_ref[...])
    o_ref[...] = x2


def _mix_ffn(x, z, wm, g, win, wout, gf, final, tm):
    n = x.shape[0]
    tok = lambda d: pl.BlockSpec((tm, d), lambda i: (i, 0))
    return pl.pallas_call(
        functools.partial(_mix_ffn_kernel, final=final),
        grid=(n // tm,),
        in_specs=[tok(D_MODEL), tok(z.shape[1]), _resident(wm.shape), _resident(g.shape),
                  _resident(win.shape), _resident(wout.shape), _resident(gf.shape)],
        out_specs=tok(D_MODEL),
        out_shape=jax.ShapeDtypeStruct((n, D_MODEL), F32),
        compiler_params=_cparams("parallel"),
        name="mix_ffn_final" if final else "mix_ffn",
    )(x, z, wm, g, win, wout, gf)


def _kvq_kernel(x_ref, gkv_ref, wdkv_ref, gckv_ref, gmix_ref, wdq_ref, gq_ref, wqn_ref, wqr_ref, wqs_ref,
                ck_ref, sk_ref, cq_ref, sq_ref,
                ckv_ref, kr_ref, ckvb_ref, krb_ref, qn_ref, qr_ref):
    x = x_ref[...]
    xn = x * lax.rsqrt(jnp.mean(x * x, axis=-1, keepdims=True) + NORM_EPS)
    ckr = _dot((xn * gkv_ref[...]).astype(BF16), wdkv_ref[...])
    ckv = _rms(ckr[:, :KV_LORA], gckv_ref[...])
    ckv_ref[...] = ckv
    ckvb_ref[...] = ckv.astype(BF16)
    kr = (ckr[:, KV_LORA:KV_LORA + QK_ROPE] * ck_ref[...]
          + ckr[:, KV_LORA + QK_ROPE:KV_LORA + 2 * QK_ROPE] * sk_ref[...])
    kr_ref[...] = kr
    krb_ref[...] = kr.astype(BF16)
    cq = _rms(_dot((xn * gmix_ref[...]).astype(BF16), wdq_ref[...]), gq_ref[...]).astype(BF16)
    qn_ref[...] = _dot(cq, wqn_ref[...]).astype(BF16)
    qr = _dot(cq, wqr_ref[...])
    qs = _dot(cq, wqs_ref[...])
    cos = cq_ref[...]
    sin = sq_ref[...]
    for j in range(MLA_HEADS * QK_ROPE // LANES):
        sl = slice(j * LANES, (j + 1) * LANES)
        qr_ref[:, sl] = (qr[:, sl] * cos + qs[:, sl] * sin).astype(BF16)


def _kvq(x, gkv, wdkv, gckv, gmix, wdq, gq, wqn, wqr, wqs, ck, sk, cq, sq, tm):
    n = x.shape[0]
    nt = ck.shape[0] // tm
    tok = lambda d: pl.BlockSpec((tm, d), lambda i: (i, 0))
    tab = lambda d: pl.BlockSpec((tm, d), lambda i: (i % nt, 0))
    res = [gkv, wdkv, gckv, gmix, wdq, gq, wqn, wqr, wqs]
    sds = lambda d, dt: jax.ShapeDtypeStruct((n, d), dt)
    return pl.pallas_call(
        _kvq_kernel,
        grid=(n // tm,),
        in_specs=[tok(D_MODEL)] + [_resident(a.shape) for a in res]
                 + [tab(QK_ROPE), tab(QK_ROPE), tab(LANES), tab(LANES)],
        out_specs=[tok(KV_LORA), tok(QK_ROPE), tok(KV_LORA), tok(QK_ROPE),
                   tok(MLA_HEADS * QK_NOPE), tok(MLA_HEADS * QK_ROPE)],
        out_shape=[sds(KV_LORA, F32), sds(QK_ROPE, F32), sds(KV_LORA, BF16), sds(QK_ROPE, BF16),
                   sds(MLA_HEADS * QK_NOPE, BF16), sds(MLA_HEADS * QK_ROPE, BF16)],
        compiler_params=_cparams("parallel"),
        name="kvq",
    )(x, *res, ck, sk, cq, sq)


def _attn_kernel(qn_ref, qr_ref, ckv_ref, krt_ref, wuk_ref, wuv_ref, o_ref,
                 q_s, qr_s, m_s, l_s, acc_s, sa_s, sb_s, *, causal, const_limit):
    tq = qn_ref.shape[0]
    kb = ckv_ref.shape[1]
    rows = q_s.shape[0]
    qscale = MLA_SCALE * LOG2E
    for g in range(MLA_HEADS // 2):
        ql = _dot(qn_ref[:, g * LANES:(g + 1) * LANES], wuk_ref[g]) * qscale
        q_s[(2 * g) * tq:(2 * g + 1) * tq, :] = ql[:, :KV_LORA].astype(BF16)
        q_s[(2 * g + 1) * tq:(2 * g + 2) * tq, :] = ql[:, KV_LORA:].astype(BF16)
    for h in range(MLA_HEADS):
        qr_s[h * tq:(h + 1) * tq, :] = (
            qr_ref[:, h * QK_ROPE:(h + 1) * QK_ROPE].astype(F32) * qscale).astype(BF16)
    m_s[...] = jnp.full(m_s.shape, NEG_INF, F32)
    l_s[...] = jnp.zeros(l_s.shape, F32)
    acc_s[...] = jnp.zeros(acc_s.shape, F32)

    lim = (pl.program_id(1) + 1) * tq if causal else const_limit
    nblk = (lim + kb - 1) // kb

    def scores(j, dst):
        dst[...] = (lax.dot_general(q_s[...], ckv_ref[j], _NT, preferred_element_type=F32)
                    + _dot(qr_s[...], krt_ref[j]))

    def softmax_pv(j, src, masked):
        kblk = ckv_ref[j]
        for r in range(rows // ATTN_ROWS):
            rs = slice(r * ATTN_ROWS, (r + 1) * ATTN_ROWS)
            s = src[rs, :]
            if masked:
                col = lax.broadcasted_iota(jnp.int32, s.shape, 1)
                s = jnp.where(col < lim - j * kb, s, NEG_INF)
            m_prev = m_s[rs, :]
            m_new = jnp.maximum(m_prev, jnp.max(s, axis=1, keepdims=True))
            alpha = jnp.exp2(m_prev - m_new)
            p = jnp.exp2(s - jnp.concatenate([m_new] * (kb // LANES), axis=1))
            psum = p[:, :LANES]
            for c in range(1, kb // LANES):
                psum = psum + p[:, c * LANES:(c + 1) * LANES]
            l_s[rs, :] = alpha * l_s[rs, :] + psum
            acc_s[rs, :] = (jnp.concatenate([alpha] * (KV_LORA // LANES), axis=1) * acc_s[rs, :]
                            + _dot(p.astype(BF16), kblk))
            m_s[rs, :] = m_new

    scores(0, sa_s)
    npairs = (nblk - 1) // 2

    def body(jj, carry):
        j = 2 * jj
        scores(j + 1, sb_s)
        softmax_pv(j, sa_s, False)
        scores(j + 2, sa_s)
        softmax_pv(j + 1, sb_s, False)
        return carry

    lax.fori_loop(0, npairs, body, 0)
    j0 = 2 * npairs

    @pl.when(nblk - j0 == 2)
    def _():
        scores(j0 + 1, sb_s)
        softmax_pv(j0, sa_s, False)
        softmax_pv(j0 + 1, sb_s, True)

    @pl.when(nblk - j0 == 1)
    def _():
        softmax_pv(j0, sa_s, True)

    o_lat = acc_s[...] / jnp.sum(l_s[...], axis=1, keepdims=True)
    for g in range(MLA_HEADS // 2):
        pair = jnp.concatenate([o_lat[(2 * g) * tq:(2 * g + 1) * tq],
                                o_lat[(2 * g + 1) * tq:(2 * g + 2) * tq]], axis=1).astype(BF16)
        o_ref[:, g * LANES:(g + 1) * LANES] = _dot(pair, wuv_ref[g]).astype(BF16)


def _attn(qn, qr, ckv_blocks, krt_blocks, wuk_pairs, wuv_pairs, b, t, causal, const_limit):
    tq = CHUNK
    nq = t // tq
    nkb = ckv_blocks.shape[1]
    rows = MLA_HEADS * tq
    qspec = lambda d: pl.BlockSpec((tq, d), lambda bi, i: (bi * nq + i, 0))
    return pl.pallas_call(
        functools.partial(_attn_kernel, causal=causal, const_limit=const_limit),
        grid=(b, nq),
        in_specs=[qspec(MLA_HEADS * QK_NOPE), qspec(MLA_HEADS * QK_ROPE),
                  pl.BlockSpec((None, nkb, KV_BLOCK, KV_LORA), lambda bi, i: (bi, 0, 0, 0)),
                  pl.BlockSpec((None, nkb, QK_ROPE, KV_BLOCK), lambda bi, i: (bi, 0, 0, 0)),
                  _resident(wuk_pairs.shape), _resident(wuv_pairs.shape)],
        out_specs=qspec(MLA_HEADS * V_HEAD),
        out_shape=jax.ShapeDtypeStruct((b * t, MLA_HEADS * V_HEAD), BF16),
        scratch_shapes=[pltpu.VMEM((rows, KV_LORA), BF16), pltpu.VMEM((rows, QK_ROPE), BF16),
                        pltpu.VMEM((rows, LANES), F32), pltpu.VMEM((rows, LANES), F32),
                        pltpu.VMEM((rows, KV_LORA), F32),
                        pltpu.VMEM((rows, KV_BLOCK), F32), pltpu.VMEM((rows, KV_BLOCK), F32)],
        compiler_params=_cparams("parallel", "arbitrary"),
        name="attn_causal" if causal else "attn_full",
    )(qn, qr, ckv_blocks, krt_blocks, wuk_pairs, wuv_pairs)


def _rope_tables(pos, half):
    inv_freq = ROPE_THETA ** (-jnp.arange(half, dtype=F32) / half)
    ang = pos.astype(F32)[:, None] * inv_freq[None, :]
    return jnp.cos(ang), jnp.sin(ang)


def _tile_rows(tab, tm):
    reps = -(-tm // tab.shape[0])
    return jnp.tile(tab, (reps, 1)) if reps > 1 else tab


def _prep_weights(g_mix, g_ffn, w_ret_in, w_ret_out, g_kv_in, w_dkv, g_ckv, w_uk, w_uv,
                  w_dq, g_q, w_uq, w_mla_out, w_ffn_in, w_ffn_out, g_final):
    half = QK_ROPE // 2
    row = lambda g: g.reshape(1, -1)
    w_kr = w_dkv[:, KV_LORA:]
    wdkv = jnp.concatenate([w_dkv, w_kr[:, half:], w_kr[:, :half]], axis=1).astype(BF16)
    wuq3 = w_uq[0].reshape(Q_LORA, MLA_HEADS, QK_NOPE + QK_ROPE)
    wqn = wuq3[:, :, :QK_NOPE].reshape(Q_LORA, -1).astype(BF16)
    wqr = wuq3[:, :, QK_NOPE:].reshape(Q_LORA, -1).astype(BF16)
    wqs = jnp.concatenate([wuq3[:, :, QK_NOPE + half:], wuq3[:, :, QK_NOPE:QK_NOPE + half]],
                          axis=2).reshape(Q_LORA, -1).astype(BF16)
    wuk_t = jnp.transpose(w_uk, (1, 2, 0)).reshape(MLA_HEADS // 2, 2, QK_NOPE, KV_LORA)
    zk = jnp.zeros_like(wuk_t[:, 0])
    wuk_pairs = jnp.concatenate([jnp.concatenate([wuk_t[:, 0], zk], axis=2),
                                 jnp.concatenate([zk, wuk_t[:, 1]], axis=2)], axis=1).astype(BF16)
    wuv_h = jnp.transpose(w_uv, (1, 0, 2)).reshape(MLA_HEADS // 2, 2, KV_LORA, V_HEAD)
    zv = jnp.zeros_like(wuv_h[:, 0])
    wuv_pairs = jnp.concatenate([jnp.concatenate([wuv_h[:, 0], zv], axis=2),
                                 jnp.concatenate([zv, wuv_h[:, 1]], axis=2)], axis=1).astype(BF16)
    return dict(
        g_mix0=row(g_mix[0]), g_mix1=row(g_mix[1]), g_ffn0=row(g_ffn[0]), g_ffn1=row(g_ffn[1]),
        w_ret_in=w_ret_in[0].astype(BF16), w_ret_out=w_ret_out[0].astype(BF16),
        g_kv_in=row(g_kv_in), wdkv=wdkv, g_ckv=row(g_ckv), wuk_pairs=wuk_pairs, wuv_pairs=wuv_pairs,
        w_dq=w_dq[0].astype(BF16), g_q=row(g_q[0]), wqn=wqn, wqr=wqr, wqs=wqs,
        w_mla_out=w_mla_out[0].astype(BF16),
        w_ffn_in0=w_ffn_in[0].astype(BF16), w_ffn_in1=w_ffn_in[1].astype(BF16),
        w_ffn_out0=w_ffn_out[0].astype(BF16), w_ffn_out1=w_ffn_out[1].astype(BF16),
        g_final=row(g_final))


def _trunk(x, pos, state0, past_ckv, past_krope, w):
    b, t, _ = x.shape
    n = b * t
    is_prompt = past_ckv is None
    xf = x.reshape(n, D_MODEL)

    cos_r, sin_r = _rope_tables(pos, RET_DK // 2)
    tm = min(TM_RET_IN, n)
    q, k, v, sg = _ret_in(xf, w["g_mix0"], w["w_ret_in"], _tile_rows(cos_r, tm), _tile_rows(sin_r, tm), tm)
    z, s_fin = _ret_scan(q, k, v, sg, state0, b, t)
    x1 = _mix_ffn(xf, z, w["w_ret_out"], w["g_ffn0"], w["w_ffn_in0"], w["w_ffn_out0"], w["g_final"],
                  False, min(TM_FFN, n))

    cos_m, sin_m = _rope_tables(pos, QK_ROPE // 2)
    ck = jnp.concatenate([cos_m, cos_m], axis=1)
    sk = jnp.concatenate([-sin_m, sin_m], axis=1)
    tm = min(TM_KVQ, n)
    reps = LANES // QK_ROPE
    ckv, kr, ckv_b, kr_b, qn, qr = _kvq(
        x1, w["g_kv_in"], w["wdkv"], w["g_ckv"], w["g_mix1"], w["w_dq"], w["g_q"],
        w["wqn"], w["wqr"], w["wqs"], _tile_rows(ck, tm), _tile_rows(sk, tm),
        _tile_rows(jnp.tile(ck, (1, reps)), tm), _tile_rows(jnp.tile(sk, (1, reps)), tm), tm)

    if is_prompt:
        keys_c = ckv_b.reshape(b, t, KV_LORA)
        keys_r = kr_b.reshape(b, t, QK_ROPE)
        limit = 0
    else:
        keys_c = jnp.concatenate([past_ckv.astype(BF16), ckv_b.reshape(b, t, KV_LORA)], axis=1)
        keys_r = jnp.concatenate([past_krope.astype(BF16), kr_b.reshape(b, t, QK_ROPE)], axis=1)
        limit = keys_c.shape[1]
    pad = (-keys_c.shape[1]) % KV_BLOCK
    if pad:
        keys_c = jnp.pad(keys_c, ((0, 0), (0, pad), (0, 0)))
        keys_r = jnp.pad(keys_r, ((0, 0), (0, pad), (0, 0)))
    nkb = keys_c.shape[1] // KV_BLOCK
    ckv_blocks = keys_c.reshape(b, nkb, KV_BLOCK, KV_LORA)
    krt_blocks = jnp.swapaxes(keys_r.reshape(b, nkb, KV_BLOCK, QK_ROPE), 2, 3)
    o = _attn(qn, qr, ckv_blocks, krt_blocks, w["wuk_pairs"], w["wuv_pairs"], b, t, is_prompt, limit)

    y = _mix_ffn(x1, o, w["w_mla_out"], w["g_ffn1"], w["w_ffn_in1"], w["w_ffn_out1"], w["g_final"],
                 True, min(TM_FFN, n))
    return (y.reshape(b, t, D_MODEL), s_fin[None], ckv.reshape(b, t, KV_LORA), kr.reshape(b, t, QK_ROPE))


def kernel(x_prompt, x_sample, state_ret, cache_ckv, cache_krope, g_mix, g_ffn, w_ret_in, w_ret_out,
           g_kv_in, w_dkv, g_ckv, w_uk, w_uv, w_dq, g_q, w_uq, w_mla_out, w_ffn_in, w_ffn_out, g_final):
    assert g_mix.shape[0] == 2 and w_ret_in.shape[0] == 1 and w_dq.shape[0] == 1
    assert x_prompt.shape[1] % RET_CHUNK == 0 and x_sample.shape[1] == CHUNK
    w = _prep_weights(g_mix, g_ffn, w_ret_in, w_ret_out, g_kv_in, w_dkv, g_ckv, w_uk, w_uv,
                      w_dq, g_q, w_uq, w_mla_out, w_ffn_in, w_ffn_out, g_final)
    past = cache_ckv.shape[1]
    pos_p = jnp.arange(x_prompt.shape[1], dtype=jnp.int32)
    pos_s = past + jnp.arange(x_sample.shape[1], dtype=jnp.int32)
    zero_state = jnp.zeros((x_prompt.shape[0], RET_HEADS, RET_DK, RET_DV), F32)
    y_p, st_p, ckv_p, kr_p = _trunk(x_prompt, pos_p, zero_state, None, None, w)
    y_s, st_s, ckv_s, kr_s = _trunk(x_sample, pos_s, state_ret[0], cache_ckv, cache_krope, w)
    return (y_p, y_s, st_p.astype(state_ret.dtype), st_s.astype(state_ret.dtype), ckv_p, kr_p, ckv_s, kr_s)
```

```python
import functools
import math

import jax
import jax.numpy as jnp
from jax import lax
from jax.experimental import pallas as pl
from jax.experimental.pallas import tpu as pltpu

F32 = jnp.float32
BF16 = jnp.bfloat16

D_MODEL = 1024
CHUNK = 64
RET_HEADS = 4
RET_DK = D_MODEL // RET_HEADS
RET_DV = 2 * D_MODEL // RET_HEADS
RET_QK = RET_HEADS * RET_DK
RET_V = RET_HEADS * RET_DV
MLA_HEADS = 16
QK_NOPE = 64
QK_ROPE = 32
V_HEAD = 64
Q_LORA = 384
KV_LORA = 256
D_FF = 2816
ROPE_THETA = 10000.0
NORM_EPS = 1e-6
MLA_SCALE = (QK_NOPE + QK_ROPE) ** -0.5
NEG_INF = -1e30

V7X_VMEM_BYTES = 64 * 2**20
VMEM_LIMIT_BYTES = V7X_VMEM_BYTES - 8 * 2**20
LANES = 128
MXU_DIM = 256

RET_CHUNK = MXU_DIM
KV_BLOCK = MXU_DIM
ATTN_ROWS = 256
ATTN_TQ = 4 * KV_BLOCK
PAIR_W = 2 * LANES
LOG2E = math.log2(math.e)
TM_RET_IN = 512
TM_FFN = 256
TM_KVQ = 256

_NT = (((1,), (1,)), ((), ()))
_TN = (((0,), (0,)), ((), ()))


def _cparams(*sem):
    return pltpu.CompilerParams(dimension_semantics=sem, vmem_limit_bytes=VMEM_LIMIT_BYTES)


def _resident(shape):
    nd = len(shape)
    return pl.BlockSpec(shape, lambda *_: (0,) * nd, pipeline_mode=pl.Buffered(1))


def _rms(x, g):
    return x * lax.rsqrt(jnp.mean(x * x, axis=-1, keepdims=True) + NORM_EPS) * g


def _dot(a, b):
    return jnp.dot(a, b, preferred_element_type=F32)


def _silu(x):
    return x * jax.nn.sigmoid(x)


def _ret_in_kernel(x_ref, g_ref, w_ref, cos_ref, sin_ref, q_ref, k_ref, v_ref, sg_ref):
    hb = _rms(x_ref[...], g_ref[...]).astype(BF16)
    cos = cos_ref[...]
    sin = sin_ref[...]
    half = RET_DK // 2
    for dst, base, scale in ((q_ref, 0, 1.0), (k_ref, RET_QK, RET_DK ** -0.5)):
        for hd in range(RET_HEADS):
            c0 = hd * RET_DK
            y = _dot(hb, w_ref[:, base + c0:base + c0 + RET_DK])
            y1 = y[:, :half]
            y2 = y[:, half:]
            dst[:, c0:c0 + half] = ((y1 * cos - y2 * sin) * scale).astype(BF16)
            dst[:, c0 + half:c0 + RET_DK] = ((y2 * cos + y1 * sin) * scale).astype(BF16)
    v_ref[...] = _dot(hb, w_ref[:, 2 * RET_QK:2 * RET_QK + RET_V]).astype(BF16)
    sg_ref[...] = _silu(_dot(hb, w_ref[:, 2 * RET_QK + RET_V:])).astype(BF16)


def _ret_in(x, g, w, cos, sin, tm):
    n = x.shape[0]
    nt = cos.shape[0] // tm
    tok = lambda d: pl.BlockSpec((tm, d), lambda i: (i, 0))
    tab = pl.BlockSpec((tm, RET_DK // 2), lambda i: (i % nt, 0))
    return pl.pallas_call(
        _ret_in_kernel,
        grid=(n // tm,),
        in_specs=[tok(D_MODEL), _resident(g.shape), _resident(w.shape), tab, tab],
        out_specs=[tok(RET_QK), tok(RET_QK), tok(RET_V), tok(RET_V)],
        out_shape=[jax.ShapeDtypeStruct((n, RET_QK), BF16), jax.ShapeDtypeStruct((n, RET_QK), BF16),
                   jax.ShapeDtypeStruct((n, RET_V), BF16), jax.ShapeDtypeStruct((n, RET_V), BF16)],
        compiler_params=_cparams("parallel"),
        name="ret_in",
    )(x, g, w, cos, sin)


def _ret_scan_kernel(q_ref, k_ref, v_ref, sg_ref, dm_ref, qd_ref, kd_ref, cd_ref, s0_ref,
                     z_ref, sout_ref, s_scr):
    c = pl.program_id(2)

    @pl.when(c == 0)
    def _():
        s_scr[...] = s0_ref[...]

    q = q_ref[...]
    k = k_ref[...]
    v = v_ref[...]
    s = s_scr[...]
    a = lax.dot_general(q, k, _NT, preferred_element_type=F32) * dm_ref[...]
    y = _dot(a.astype(BF16), v) + qd_ref[...] * _dot(q, s.astype(BF16))
    kd = (k.astype(F32) * kd_ref[...]).astype(BF16)
    s_new = s * cd_ref[...] + lax.dot_general(kd, v, _TN, preferred_element_type=F32)
    s_scr[...] = s_new
    yn = y * lax.rsqrt(jnp.mean(y * y, axis=-1, keepdims=True) + NORM_EPS)
    z_ref[...] = (sg_ref[...].astype(F32) * yn).astype(BF16)

    @pl.when(c == pl.num_programs(2) - 1)
    def _():
        sout_ref[...] = s_new


def _ret_scan(q, k, v, sg, state0, b, t):
    ck = min(RET_CHUNK, t)
    nc = t // ck
    lg = jnp.log(1.0 - 2.0 ** (-5.0 - jnp.arange(RET_HEADS, dtype=F32)))
    idx = jnp.arange(ck, dtype=F32)
    diff = idx[:, None] - idx[None, :]
    dmask = jnp.where(diff >= 0, jnp.exp(jnp.maximum(diff, 0.0)[None] * lg[:, None, None]), 0.0)
    qdec = jnp.exp((idx + 1.0)[None, :] * lg[:, None])[:, :, None]
    kdec = jnp.exp((ck - 1.0 - idx)[None, :] * lg[:, None])[:, :, None]
    cdec = jnp.broadcast_to(jnp.exp(ck * lg)[:, None, None], (RET_HEADS, 1, RET_DV))

    qk_spec = pl.BlockSpec((None, ck, RET_DK), lambda bi, h, c: (bi, c, h))
    v_spec = pl.BlockSpec((None, ck, RET_DV), lambda bi, h, c: (bi, c, h))
    head = lambda r, w: pl.BlockSpec((None, r, w), lambda bi, h, c: (h, 0, 0))
    st_spec = pl.BlockSpec((None, None, RET_DK, RET_DV), lambda bi, h, c: (bi, h, 0, 0))
    z, s_out = pl.pallas_call(
        _ret_scan_kernel,
        grid=(b, RET_HEADS, nc),
        in_specs=[qk_spec, qk_spec, v_spec, v_spec, head(ck, ck), head(ck, 1), head(ck, 1),
                  head(1, RET_DV), st_spec],
        out_specs=[v_spec, st_spec],
        out_shape=[jax.ShapeDtypeStruct((b, t, RET_V), BF16),
                   jax.ShapeDtypeStruct((b, RET_HEADS, RET_DK, RET_DV), F32)],
        scratch_shapes=[pltpu.VMEM((RET_DK, RET_DV), F32)],
        compiler_params=_cparams("parallel", "parallel", "arbitrary"),
        name="ret_scan",
    )(q.reshape(b, t, RET_QK), k.reshape(b, t, RET_QK), v.reshape(b, t, RET_V), sg.reshape(b, t, RET_V),
      dmask, qdec, kdec, cdec, state0)
    return z.reshape(b * t, RET_V), s_out


def _mix_ffn_kernel(x_ref, z_ref, wm_ref, g_ref, win_ref, wout_ref, gf_ref, o_ref, *, final):
    x1 = x_ref[...] + _dot(z_ref[...], wm_ref[...])
    hb = _rms(x1, g_ref[...]).astype(BF16)
    gu = _dot(hb, win_ref[...])
    a = (_silu(gu[:, :D_FF]) * gu[:, D_FF:]).astype(BF16)
    x2 = x1 + _dot(a, wout_ref[...])
    if final:
        x2 = _rms(x2, gf_ref[...])
    o_ref[...] = x2


def _mix_ffn(x, z, wm, g, win, wout, gf, final, tm):
    n = x.shape[0]
    tok = lambda d: pl.BlockSpec((tm, d), lambda i: (i, 0))
    return pl.pallas_call(
        functools.partial(_mix_ffn_kernel, final=final),
        grid=(n // tm,),
        in_specs=[tok(D_MODEL), tok(z.shape[1]), _resident(wm.shape), _resident(g.shape),
                  _resident(win.shape), _resident(wout.shape), _resident(gf.shape)],
        out_specs=tok(D_MODEL),
        out_shape=jax.ShapeDtypeStruct((n, D_MODEL), F32),
        compiler_params=_cparams("parallel"),
        name="mix_ffn_final" if final else "mix_ffn",
    )(x, z, wm, g, win, wout, gf)


def _kvq_kernel(x_ref, gkv_ref, wdkv_ref, gckv_ref, gmix_ref, wdq_ref, gq_ref, wq1_ref, wq2_ref,
                wuk_ref, wuv_ref, cos_ref, sin_ref,
                ckv_ref, kr_ref, ckvb_ref, krb_ref, q_ref, *kv_refs, expand):
    x = x_ref[...]
    xn = x * lax.rsqrt(jnp.mean(x * x, axis=-1, keepdims=True) + NORM_EPS)
    cos = cos_ref[...]
    sin = sin_ref[...]
    ckr = _dot((xn * gkv_ref[...]).astype(BF16), wdkv_ref[...])
    ckv = _rms(ckr[:, :KV_LORA], gckv_ref[...])
    ckvb = ckv.astype(BF16)
    ckv_ref[...] = ckv
    ckvb_ref[...] = ckvb
    krot = ckr[:, KV_LORA:KV_LORA + LANES] * cos + ckr[:, KV_LORA + LANES:] * sin
    kr_ref[...] = krot[:, :QK_ROPE]
    krb_ref[...] = krot[:, :QK_ROPE].astype(BF16)
    qscale = MLA_SCALE * LOG2E
    cq = _rms(_dot((xn * gmix_ref[...]).astype(BF16), wdq_ref[...]), gq_ref[...]).astype(BF16)
    y1 = _dot(cq, wq1_ref[...]) * qscale
    y2 = _dot(cq, wq2_ref[...]) * qscale
    for p in range(MLA_HEADS // 2):
        lo = p * PAIR_W
        rot = y1[:, lo + LANES:lo + PAIR_W] * cos + y2[:, p * LANES:(p + 1) * LANES] * sin
        qg = jnp.concatenate([y1[:, lo:lo + LANES], rot], axis=1).astype(BF16)
        if expand:
            q_ref[p] = qg
        else:
            q_ref[:, lo:lo + PAIR_W] = qg
    if expand:
        k_ref, v_ref = kv_refs
        kn = _dot(ckvb, wuk_ref[...])
        vv = _dot(ckvb, wuv_ref[...])
        krb = krot.astype(BF16)
        for p in range(MLA_HEADS // 2):
            k_ref[p] = jnp.concatenate([kn[:, p * LANES:(p + 1) * LANES].astype(BF16), krb], axis=1)
            v_ref[p] = vv[:, p * LANES:(p + 1) * LANES].astype(BF16)


def _kvq(x, w, cos, sin, b, t, tm, expand):
    n = b * t
    nt = cos.shape[0] // tm
    npairs = MLA_HEADS // 2
    tok = lambda d: pl.BlockSpec((tm, d), lambda i: (i, 0))
    tab = pl.BlockSpec((tm, LANES), lambda i: (i % nt, 0))
    res = [w["g_kv_in"], w["wdkv"], w["g_ckv"], w["g_mix1"], w["w_dq"], w["g_q"], w["wq1"], w["wq2"],
           w["wuk_all"], w["wuv_all"]]
    sds = lambda d, dt: jax.ShapeDtypeStruct((n, d), dt)
    out_specs = [tok(KV_LORA), tok(QK_ROPE), tok(KV_LORA), tok(QK_ROPE)]
    out_shape = [sds(KV_LORA, F32), sds(QK_ROPE, F32), sds(KV_LORA, BF16), sds(QK_ROPE, BF16)]
    if expand:
        tpb = t // tm
        pm = lambda d: pl.BlockSpec((None, npairs, tm, d), lambda i: (i // tpb, 0, i % tpb, 0))
        pm_sds = lambda d: jax.ShapeDtypeStruct((b, npairs, t, d), BF16)
        out_specs += [pm(PAIR_W), pm(PAIR_W), pm(LANES)]
        out_shape += [pm_sds(PAIR_W), pm_sds(PAIR_W), pm_sds(LANES)]
    else:
        out_specs += [tok(npairs * PAIR_W)]
        out_shape += [sds(npairs * PAIR_W, BF16)]
    return pl.pallas_call(
        functools.partial(_kvq_kernel, expand=expand),
        grid=(n // tm,),
        in_specs=[tok(D_MODEL)] + [_resident(a.shape) for a in res] + [tab, tab],
        out_specs=out_specs,
        out_shape=out_shape,
        compiler_params=_cparams("parallel"),
        name="kvq_expand" if expand else "kvq",
    )(x, *res, cos, sin)


def _attn_heads_kernel(q_ref, k_ref, v_ref, o_ref, q_s, m_s, l_s, acc_s, sa_s, sb_s):
    tq = q_ref.shape[0]
    kb = k_ref.shape[1]
    rows = 2 * tq
    i = pl.program_id(2)
    q = q_ref[...].astype(F32)
    grp = lax.broadcasted_iota(jnp.int32, q.shape, 1) // QK_ROPE
    zero = jnp.zeros_like(q)
    q_s[0:tq, :] = jnp.where(grp < 2, q, jnp.where(grp == 4, q, zero)).astype(BF16)
    q_s[tq:rows, :] = jnp.where(grp < 2, zero, jnp.where(grp < 4, q, jnp.where(grp == 5, q, zero))).astype(BF16)
    m_s[...] = jnp.full(m_s.shape, NEG_INF, F32)
    l_s[...] = jnp.zeros(l_s.shape, F32)
    acc_s[...] = jnp.zeros(acc_s.shape, F32)

    def scores(j, dst):
        dst[...] = lax.dot_general(q_s[...], k_ref[j], _NT, preferred_element_type=F32)

    def softmax_pv(j, src, diag):
        vblk = v_ref[j]
        for r in range(rows // ATTN_ROWS):
            rs = slice(r * ATTN_ROWS, (r + 1) * ATTN_ROWS)
            s = src[rs, :]
            if diag is not None:
                qc0 = ((r * ATTN_ROWS) % tq) // CHUNK
                qc1 = qc0 + ATTN_ROWS // CHUNK - 1
                kc0 = diag * (kb // CHUNK)
                kc1 = kc0 + kb // CHUNK - 1
                if kc0 > qc1:
                    continue
                if kc1 > qc0:
                    qc = qc0 + lax.broadcasted_iota(jnp.int32, s.shape, 0) // CHUNK
                    kc = kc0 + lax.broadcasted_iota(jnp.int32, s.shape, 1) // CHUNK
                    s = jnp.where(kc <= qc, s, NEG_INF)
            m_prev = m_s[rs, :]
            m_new = jnp.maximum(m_prev, jnp.max(s, axis=1, keepdims=True))
            alpha = jnp.exp2(m_prev - m_new)
            p = jnp.exp2(s - jnp.concatenate([m_new] * (kb // LANES), axis=1))
            psum = p[:, :LANES]
            for c in range(1, kb // LANES):
                psum = psum + p[:, c * LANES:(c + 1) * LANES]
            l_s[rs, :] = alpha * l_s[rs, :] + psum
            acc_s[rs, :] = alpha * acc_s[rs, :] + _dot(p.astype(BF16), vblk)
            m_s[rs, :] = m_new

    nb = tq // kb
    bufs = (sa_s, sb_s)
    scores(0, sa_s)

    def body(jj, carry):
        j = nb * jj
        for d in range(nb):
            scores(j + d + 1, bufs[(d + 1) % 2])
            softmax_pv(j + d, bufs[d % 2], None)
        return carry

    lax.fori_loop(0, i, body, 0)
    j0 = nb * i
    for d in range(nb):
        if d + 1 < nb:
            scores(j0 + d + 1, bufs[(d + 1) % 2])
        softmax_pv(j0 + d, bufs[d % 2], d)

    o = acc_s[...] / jnp.sum(l_s[...], axis=1, keepdims=True)
    lane = lax.broadcasted_iota(jnp.int32, (tq, LANES), 1)
    o_ref[...] = jnp.where(lane < V_HEAD, o[0:tq, :], o[tq:rows, :]).astype(BF16)


def _attn_heads(qcat, kcat, vcat, b, t):
    tq = ATTN_TQ
    assert tq % (2 * KV_BLOCK) == 0 and t % tq == 0
    nq = t // tq
    nkb = t // KV_BLOCK
    npairs = MLA_HEADS // 2
    rows = 2 * tq
    kv = lambda d: pl.BlockSpec((None, None, nkb, KV_BLOCK, d), lambda bi, p, i: (bi, p, 0, 0, 0))
    return pl.pallas_call(
        _attn_heads_kernel,
        grid=(b, npairs, nq),
        in_specs=[pl.BlockSpec((None, None, tq, PAIR_W), lambda bi, p, i: (bi, p, i, 0)),
                  kv(PAIR_W), kv(LANES)],
        out_specs=pl.BlockSpec((tq, LANES), lambda bi, p, i: (bi * nq + i, p)),
        out_shape=jax.ShapeDtypeStruct((b * t, MLA_HEADS * V_HEAD), BF16),
        scratch_shapes=[pltpu.VMEM((rows, PAIR_W), BF16),
                        pltpu.VMEM((rows, LANES), F32), pltpu.VMEM((rows, LANES), F32),
                        pltpu.VMEM((rows, LANES), F32),
                        pltpu.VMEM((rows, KV_BLOCK), F32), pltpu.VMEM((rows, KV_BLOCK), F32)],
        compiler_params=_cparams("parallel", "parallel", "arbitrary"),
        name="attn_heads",
    )(qcat, kcat.reshape(b, npairs, nkb, KV_BLOCK, PAIR_W), vcat.reshape(b, npairs, nkb, KV_BLOCK, LANES))


def _attn_kernel(q_ref, ckv_ref, krt_ref, wuk_ref, wuv_ref, o_ref,
                 q_s, qr_s, m_s, l_s, acc_s, sa_s, sb_s, *, lim):
    tq = q_ref.shape[0]
    kb = ckv_ref.shape[1]
    rows = q_s.shape[0]
    for g in range(MLA_HEADS // 2):
        lo = g * PAIR_W
        ql = _dot(q_ref[:, lo:lo + LANES], wuk_ref[g])
        q_s[(2 * g) * tq:(2 * g + 1) * tq, :] = ql[:, :KV_LORA].astype(BF16)
        q_s[(2 * g + 1) * tq:(2 * g + 2) * tq, :] = ql[:, KV_LORA:].astype(BF16)
        for e in range(2):
            qr_s[(2 * g + e) * tq:(2 * g + e + 1) * tq, :] = (
                q_ref[:, lo + LANES + e * QK_ROPE:lo + LANES + (e + 1) * QK_ROPE])
    m_s[...] = jnp.full(m_s.shape, NEG_INF, F32)
    l_s[...] = jnp.zeros(l_s.shape, F32)
    acc_s[...] = jnp.zeros(acc_s.shape, F32)

    nblk = (lim + kb - 1) // kb

    def scores(j, dst):
        dst[...] = (lax.dot_general(q_s[...], ckv_ref[j], _NT, preferred_element_type=F32)
                    + _dot(qr_s[...], krt_ref[j]))

    def softmax_pv(j, src, masked):
        kblk = ckv_ref[j]
        for r in range(rows // ATTN_ROWS):
            rs = slice(r * ATTN_ROWS, (r + 1) * ATTN_ROWS)
            s = src[rs, :]
            if masked:
                col = lax.broadcasted_iota(jnp.int32, s.shape, 1)
                s = jnp.where(col < lim - j * kb, s, NEG_INF)
            m_prev = m_s[rs, :]
            m_new = jnp.maximum(m_prev, jnp.max(s, axis=1, keepdims=True))
            alpha = jnp.exp2(m_prev - m_new)
            p = jnp.exp2(s - jnp.concatenate([m_new] * (kb // LANES), axis=1))
            psum = p[:, :LANES]
            for c in range(1, kb // LANES):
                psum = psum + p[:, c * LANES:(c + 1) * LANES]
            l_s[rs, :] = alpha * l_s[rs, :] + psum
            acc_s[rs, :] = (jnp.concatenate([alpha] * (KV_LORA // LANES), axis=1) * acc_s[rs, :]
                            + _dot(p.astype(BF16), kblk))
            m_s[rs, :] = m_new

    scores(0, sa_s)
    npairs = (nblk - 1) // 2

    def body(jj, carry):
        j = 2 * jj
        scores(j + 1, sb_s)
        softmax_pv(j, sa_s, False)
        scores(j + 2, sa_s)
        softmax_pv(j + 1, sb_s, False)
        return carry

    lax.fori_loop(0, npairs, body, 0)
    j0 = 2 * npairs
    if nblk - j0 == 2:
        scores(j0 + 1, sb_s)
        softmax_pv(j0, sa_s, False)
        softmax_pv(j0 + 1, sb_s, True)
    else:
        softmax_pv(j0, sa_s, True)

    o_lat = acc_s[...] / jnp.sum(l_s[...], axis=1, keepdims=True)
    for g in range(MLA_HEADS // 2):
        pair = jnp.concatenate([o_lat[(2 * g) * tq:(2 * g + 1) * tq],
                                o_lat[(2 * g + 1) * tq:(2 * g + 2) * tq]], axis=1).astype(BF16)
        o_ref[:, g * LANES:(g + 1) * LANES] = _dot(pair, wuv_ref[g]).astype(BF16)


def _attn(qcat, ckv_blocks, krt_blocks, wuk_pairs, wuv_pairs, b, t, lim):
    tq = CHUNK
    nq = t // tq
    nkb = ckv_blocks.shape[1]
    rows = MLA_HEADS * tq
    qspec = lambda d: pl.BlockSpec((tq, d), lambda bi, i: (bi * nq + i, 0))
    return pl.pallas_call(
        functools.partial(_attn_kernel, lim=lim),
        grid=(b, nq),
        in_specs=[qspec(qcat.shape[1]),
                  pl.BlockSpec((None, nkb, KV_BLOCK, KV_LORA), lambda bi, i: (bi, 0, 0, 0)),
                  pl.BlockSpec((None, nkb, QK_ROPE, KV_BLOCK), lambda bi, i: (bi, 0, 0, 0)),
                  _resident(wuk_pairs.shape), _resident(wuv_pairs.shape)],
        out_specs=qspec(MLA_HEADS * V_HEAD),
        out_shape=jax.ShapeDtypeStruct((b * t, MLA_HEADS * V_HEAD), BF16),
        scratch_shapes=[pltpu.VMEM((rows, KV_LORA), BF16), pltpu.VMEM((rows, QK_ROPE), BF16),
                        pltpu.VMEM((rows, LANES), F32), pltpu.VMEM((rows, LANES), F32),
                        pltpu.VMEM((rows, KV_LORA), F32),
                        pltpu.VMEM((rows, KV_BLOCK), F32), pltpu.VMEM((rows, KV_BLOCK), F32)],
        compiler_params=_cparams("parallel", "arbitrary"),
        name="attn_full",
    )(qcat, ckv_blocks, krt_blocks, wuk_pairs, wuv_pairs)


def _rope_tables(pos, half):
    inv_freq = ROPE_THETA ** (-jnp.arange(half, dtype=F32) / half)
    ang = pos.astype(F32)[:, None] * inv_freq[None, :]
    return jnp.cos(ang), jnp.sin(ang)


def _tile_rows(tab, tm):
    reps = -(-tm // tab.shape[0])
    return jnp.tile(tab, (reps, 1)) if reps > 1 else tab


def _prep_weights(g_mix, g_ffn, w_ret_in, w_ret_out, g_kv_in, w_dkv, g_ckv, w_uk, w_uv,
                  w_dq, g_q, w_uq, w_mla_out, w_ffn_in, w_ffn_out, g_final):
    half = QK_ROPE // 2
    row = lambda g: g.reshape(1, -1)
    npairs = MLA_HEADS // 2
    swap = lambda a: jnp.concatenate([a[..., half:], a[..., :half]], axis=-1)
    w_kr = w_dkv[:, KV_LORA:]
    zk64 = jnp.zeros((D_MODEL, LANES - 2 * QK_ROPE), F32)
    wdkv = jnp.concatenate([w_dkv[:, :KV_LORA], w_kr, w_kr, zk64, swap(w_kr), swap(w_kr), zk64],
                           axis=1).astype(BF16)
    wuq4 = w_uq[0].reshape(Q_LORA, npairs, 2, QK_NOPE + QK_ROPE)
    nope = wuq4[..., :QK_NOPE].reshape(Q_LORA, npairs, 2 * QK_NOPE)
    rope = wuq4[..., QK_NOPE:]
    zq64 = jnp.zeros((Q_LORA, npairs, LANES - 2 * QK_ROPE), F32)
    wq1 = jnp.concatenate([nope, rope.reshape(Q_LORA, npairs, 2 * QK_ROPE), zq64],
                          axis=2).reshape(Q_LORA, npairs * PAIR_W).astype(BF16)
    wq2 = jnp.concatenate([swap(rope).reshape(Q_LORA, npairs, 2 * QK_ROPE), zq64],
                          axis=2).reshape(Q_LORA, npairs * LANES).astype(BF16)
    wuk_t = jnp.transpose(w_uk, (1, 2, 0)).reshape(MLA_HEADS // 2, 2, QK_NOPE, KV_LORA)
    zk = jnp.zeros_like(wuk_t[:, 0])
    wuk_pairs = jnp.concatenate([jnp.concatenate([wuk_t[:, 0], zk], axis=2),
                                 jnp.concatenate([zk, wuk_t[:, 1]], axis=2)], axis=1).astype(BF16)
    wuv_h = jnp.transpose(w_uv, (1, 0, 2)).reshape(MLA_HEADS // 2, 2, KV_LORA, V_HEAD)
    zv = jnp.zeros_like(wuv_h[:, 0])
    wuv_pairs = jnp.concatenate([jnp.concatenate([wuv_h[:, 0], zv], axis=2),
                                 jnp.concatenate([zv, wuv_h[:, 1]], axis=2)], axis=1).astype(BF16)
    return dict(
        g_mix0=row(g_mix[0]), g_mix1=row(g_mix[1]), g_ffn0=row(g_ffn[0]), g_ffn1=row(g_ffn[1]),
        w_ret_in=w_ret_in[0].astype(BF16), w_ret_out=w_ret_out[0].astype(BF16),
        g_kv_in=row(g_kv_in), wdkv=wdkv, g_ckv=row(g_ckv), wuk_pairs=wuk_pairs, wuv_pairs=wuv_pairs,
        w_dq=w_dq[0].astype(BF16), g_q=row(g_q[0]), wq1=wq1, wq2=wq2,
        wuk_all=w_uk.reshape(KV_LORA, -1).astype(BF16), wuv_all=w_uv.reshape(KV_LORA, -1).astype(BF16),
        w_mla_out=w_mla_out[0].astype(BF16),
        w_ffn_in0=w_ffn_in[0].astype(BF16), w_ffn_in1=w_ffn_in[1].astype(BF16),
        w_ffn_out0=w_ffn_out[0].astype(BF16), w_ffn_out1=w_ffn_out[1].astype(BF16),
        g_final=row(g_final))


def _trunk(x, pos, state0, past_ckv, past_krope, w):
    b, t, _ = x.shape
    n = b * t
    is_prompt = past_ckv is None
    xf = x.reshape(n, D_MODEL)

    cos_r, sin_r = _rope_tables(pos, RET_DK // 2)
    tm = min(TM_RET_IN, n)
    q, k, v, sg = _ret_in(xf, w["g_mix0"], w["w_ret_in"], _tile_rows(cos_r, tm), _tile_rows(sin_r, tm), tm)
    z, s_fin = _ret_scan(q, k, v, sg, state0, b, t)
    x1 = _mix_ffn(xf, z, w["w_ret_out"], w["g_ffn0"], w["w_ffn_in0"], w["w_ffn_out0"], w["g_final"],
                  False, min(TM_FFN, n))

    cos_m, sin_m = _rope_tables(pos, QK_ROPE // 2)
    zpad = jnp.zeros((t, LANES - 2 * QK_ROPE), F32)
    cos_g = jnp.concatenate([cos_m, cos_m, cos_m, cos_m, zpad], axis=1)
    sin_g = jnp.concatenate([-sin_m, sin_m, -sin_m, sin_m, zpad], axis=1)
    tm = min(TM_KVQ, n)
    outs = _kvq(x1, w, _tile_rows(cos_g, tm), _tile_rows(sin_g, tm), b, t, tm, is_prompt)
    ckv, kr, ckv_b, kr_b, qcat = outs[:5]

    if is_prompt:
        o = _attn_heads(qcat, outs[5], outs[6], b, t)
    else:
        keys_c = jnp.concatenate([past_ckv.astype(BF16), ckv_b.reshape(b, t, KV_LORA)], axis=1)
        keys_r = jnp.concatenate([past_krope.astype(BF16), kr_b.reshape(b, t, QK_ROPE)], axis=1)
        lim = keys_c.shape[1]
        pad = (-lim) % KV_BLOCK
        if pad:
            keys_c = jnp.pad(keys_c, ((0, 0), (0, pad), (0, 0)))
            keys_r = jnp.pad(keys_r, ((0, 0), (0, pad), (0, 0)))
        nkb = keys_c.shape[1] // KV_BLOCK
        ckv_blocks = keys_c.reshape(b, nkb, KV_BLOCK, KV_LORA)
        krt_blocks = jnp.swapaxes(keys_r.reshape(b, nkb, KV_BLOCK, QK_ROPE), 2, 3)
        o = _attn(qcat, ckv_blocks, krt_blocks, w["wuk_pairs"], w["wuv_pairs"], b, t, lim)

    y = _mix_ffn(x1, o, w["w_mla_out"], w["g_ffn1"], w["w_ffn_in1"], w["w_ffn_out1"], w["g_final"],
                 True, min(TM_FFN, n))
    return (y.reshape(b, t, D_MODEL), s_fin[None], ckv.reshape(b, t, KV_LORA), kr.reshape(b, t, QK_ROPE))


def kernel(x_prompt, x_sample, state_ret, cache_ckv, cache_krope, g_mix, g_ffn, w_ret_in, w_ret_out,
           g_kv_in, w_dkv, g_ckv, w_uk, w_uv, w_dq, g_q, w_uq, w_mla_out, w_ffn_in, w_ffn_out, g_final):
    assert g_mix.shape[0] == 2 and w_ret_in.shape[0] == 1 and w_dq.shape[0] == 1
    assert x_prompt.shape[1] % ATTN_TQ == 0 and x_sample.shape[1] == CHUNK
    w = _prep_weights(g_mix, g_ffn, w_ret_in, w_ret_out, g_kv_in, w_dkv, g_ckv, w_uk, w_uv,
                      w_dq, g_q, w_uq, w_mla_out, w_ffn_in, w_ffn_out, g_final)
    past = cache_ckv.shape[1]
    pos_p = jnp.arange(x_prompt.shape[1], dtype=jnp.int32)
    pos_s = past + jnp.arange(x_sample.shape[1], dtype=jnp.int32)
    zero_state = jnp.zeros((x_prompt.shape[0], RET_HEADS, RET_DK, RET_DV), F32)
    y_p, st_p, ckv_p, kr_p = _trunk(x_prompt, pos_p, zero_state, None, None, w)
    y_s, st_s, ckv_s, kr_s = _trunk(x_sample, pos_s, state_ret[0], cache_ckv, cache_krope, w)
    return (y_p, y_s, st_p.astype(state_ret.dtype), st_s.astype(state_ret.dtype), ckv_p, kr_p, ckv_s, kr_s)
```

```python
import functools
import math

import jax
import jax.numpy as jnp
from jax import lax
from jax.experimental import pallas as pl
from jax.experimental.pallas import tpu as pltpu

F32 = jnp.float32
BF16 = jnp.bfloat16

D_MODEL = 1024
CHUNK = 64
RET_HEADS = 4
RET_DK = D_MODEL // RET_HEADS
RET_DV = 2 * D_MODEL // RET_HEADS
RET_QK = RET_HEADS * RET_DK
RET_V = RET_HEADS * RET_DV
MLA_HEADS = 16
QK_NOPE = 64
QK_ROPE = 32
V_HEAD = 64
Q_LORA = 384
KV_LORA = 256
D_FF = 2816
ROPE_THETA = 10000.0
NORM_EPS = 1e-6
MLA_SCALE = (QK_NOPE + QK_ROPE) ** -0.5
NEG_INF = -1e30

V7X_VMEM_BYTES = 64 * 2**20
VMEM_LIMIT_BYTES = V7X_VMEM_BYTES - 8 * 2**20
LANES = 128
MXU_DIM = 256

RET_CHUNK = MXU_DIM
RET_HEAD_GROUP = 4
KV_BLOCK = MXU_DIM
ATTN_ROWS = 256
ATTN_TQ = 8 * KV_BLOCK
PAIR_W = 2 * LANES
LOG2E = math.log2(math.e)
TM_RET_IN = 512
TM_FFN = 256
TM_KVQ = 256

_NT = (((1,), (1,)), ((), ()))
_TN = (((0,), (0,)), ((), ()))


def _cparams(*sem):
    return pltpu.CompilerParams(dimension_semantics=sem, vmem_limit_bytes=VMEM_LIMIT_BYTES)


def _resident(shape):
    nd = len(shape)
    return pl.BlockSpec(shape, lambda *_: (0,) * nd, pipeline_mode=pl.Buffered(1))


def _rms(x, g):
    return x * lax.rsqrt(jnp.mean(x * x, axis=-1, keepdims=True) + NORM_EPS) * g


def _dot(a, b):
    return jnp.dot(a, b, preferred_element_type=F32)


def _silu(x):
    return x * jax.nn.sigmoid(x)


def _ret_in_kernel(x_ref, g_ref, w_ref, cos_ref, sin_ref, q_ref, k_ref, v_ref, sg_ref):
    hb = _rms(x_ref[...], g_ref[...]).astype(BF16)
    cos = cos_ref[...]
    sin = sin_ref[...]
    half = RET_DK // 2
    for dst, base, scale in ((q_ref, 0, 1.0), (k_ref, RET_QK, RET_DK ** -0.5)):
        for hd in range(RET_HEADS):
            c0 = hd * RET_DK
            y = _dot(hb, w_ref[:, base + c0:base + c0 + RET_DK])
            y1 = y[:, :half]
            y2 = y[:, half:]
            dst[:, c0:c0 + half] = ((y1 * cos - y2 * sin) * scale).astype(BF16)
            dst[:, c0 + half:c0 + RET_DK] = ((y2 * cos + y1 * sin) * scale).astype(BF16)
    v_ref[...] = _dot(hb, w_ref[:, 2 * RET_QK:2 * RET_QK + RET_V]).astype(BF16)
    sg_ref[...] = _silu(_dot(hb, w_ref[:, 2 * RET_QK + RET_V:])).astype(BF16)


def _ret_in(x, g, w, cos, sin, tm):
    n = x.shape[0]
    nt = cos.shape[0] // tm
    tok = lambda d: pl.BlockSpec((tm, d), lambda i: (i, 0))
    tab = pl.BlockSpec((tm, RET_DK // 2), lambda i: (i % nt, 0))
    return pl.pallas_call(
        _ret_in_kernel,
        grid=(n // tm,),
        in_specs=[tok(D_MODEL), _resident(g.shape), _resident(w.shape), tab, tab],
        out_specs=[tok(RET_QK), tok(RET_QK), tok(RET_V), tok(RET_V)],
        out_shape=[jax.ShapeDtypeStruct((n, RET_QK), BF16), jax.ShapeDtypeStruct((n, RET_QK), BF16),
                   jax.ShapeDtypeStruct((n, RET_V), BF16), jax.ShapeDtypeStruct((n, RET_V), BF16)],
        compiler_params=_cparams("parallel"),
        name="ret_in",
    )(x, g, w, cos, sin)


def _ret_scan_kernel(q_ref, k_ref, v_ref, sg_ref, dm_ref, qd_ref, kd_ref, cd_ref, s0_ref,
                     z_ref, sout_ref, s_scr):
    c = pl.program_id(2)

    @pl.when(c == 0)
    def _():
        s_scr[...] = s0_ref[...]

    for h in range(s_scr.shape[0]):
        qs = slice(h * RET_DK, (h + 1) * RET_DK)
        vs = slice(h * RET_DV, (h + 1) * RET_DV)
        q = q_ref[:, qs]
        k = k_ref[:, qs]
        v = v_ref[:, vs]
        s = s_scr[h]
        a = lax.dot_general(q, k, _NT, preferred_element_type=F32) * dm_ref[h]
        y = _dot(a.astype(BF16), v) + qd_ref[h] * _dot(q, s.astype(BF16))
        kd = (k.astype(F32) * kd_ref[h]).astype(BF16)
        s_scr[h] = s * cd_ref[h] + lax.dot_general(kd, v, _TN, preferred_element_type=F32)
        yn = y * lax.rsqrt(jnp.mean(y * y, axis=-1, keepdims=True) + NORM_EPS)
        z_ref[:, vs] = (sg_ref[:, vs].astype(F32) * yn).astype(BF16)

    @pl.when(c == pl.num_programs(2) - 1)
    def _():
        sout_ref[...] = s_scr[...]


def _ret_scan(q, k, v, sg, state0, b, t):
    ck = min(RET_CHUNK, t)
    nc = t // ck
    lg = jnp.log(1.0 - 2.0 ** (-5.0 - jnp.arange(RET_HEADS, dtype=F32)))
    idx = jnp.arange(ck, dtype=F32)
    diff = idx[:, None] - idx[None, :]
    dmask = jnp.where(diff >= 0, jnp.exp(jnp.maximum(diff, 0.0)[None] * lg[:, None, None]), 0.0)
    qdec = jnp.exp((idx + 1.0)[None, :] * lg[:, None])[:, :, None]
    kdec = jnp.exp((ck - 1.0 - idx)[None, :] * lg[:, None])[:, :, None]
    cdec = jnp.broadcast_to(jnp.exp(ck * lg)[:, None, None], (RET_HEADS, 1, RET_DV))

    hg = RET_HEAD_GROUP
    qk_spec = pl.BlockSpec((None, ck, hg * RET_DK), lambda bi, h, c: (bi, c, h))
    v_spec = pl.BlockSpec((None, ck, hg * RET_DV), lambda bi, h, c: (bi, c, h))
    head = lambda r, w: pl.BlockSpec((hg, r, w), lambda bi, h, c: (h, 0, 0))
    st_spec = pl.BlockSpec((None, hg, RET_DK, RET_DV), lambda bi, h, c: (bi, h, 0, 0))
    z, s_out = pl.pallas_call(
        _ret_scan_kernel,
        grid=(b, RET_HEADS // hg, nc),
        in_specs=[qk_spec, qk_spec, v_spec, v_spec, head(ck, ck), head(ck, 1), head(ck, 1),
                  head(1, RET_DV), st_spec],
        out_specs=[v_spec, st_spec],
        out_shape=[jax.ShapeDtypeStruct((b, t, RET_V), BF16),
                   jax.ShapeDtypeStruct((b, RET_HEADS, RET_DK, RET_DV), F32)],
        scratch_shapes=[pltpu.VMEM((hg, RET_DK, RET_DV), F32)],
        compiler_params=_cparams("parallel", "parallel", "arbitrary"),
        name="ret_scan",
    )(q.reshape(b, t, RET_QK), k.reshape(b, t, RET_QK), v.reshape(b, t, RET_V), sg.reshape(b, t, RET_V),
      dmask, qdec, kdec, cdec, state0)
    return z.reshape(b * t, RET_V), s_out


def _mix_ffn_kernel(x_ref, z_ref, wm_ref, g_ref, win_ref, wout_ref, gf_ref, o_ref, *, final):
    x1 = x_ref[...] + _dot(z_ref[...], wm_ref[...])
    hb = _rms(x1, g_ref[...]).astype(BF16)
    gu = _dot(hb, win_ref[...])
    a = (_silu(gu[:, :D_FF]) * gu[:, D_FF:]).astype(BF16)
    x2 = x1 + _dot(a, wout_ref[...])
    if final:
        x2 = _rms(x2, gf_ref[...])
    o_ref[...] = x2


def _mix_ffn(x, z, wm, g, win, wout, gf, final, tm):
    n = x.shape[0]
    tok = lambda d: pl.BlockSpec((tm, d), lambda i: (i, 0))
    return pl.pallas_call(
        functools.partial(_mix_ffn_kernel, final=final),
        grid=(n // tm,),
        in_specs=[tok(D_MODEL), tok(z.shape[1]), _resident(wm.shape), _resident(g.shape),
                  _resident(win.shape), _resident(wout.shape), _resident(gf.shape)],
        out_specs=tok(D_MODEL),
        out_shape=jax.ShapeDtypeStruct((n, D_MODEL), F32),
        compiler_params=_cparams("parallel"),
        name="mix_ffn_final" if final else "mix_ffn",
    )(x, z, wm, g, win, wout, gf)


def _kvq_kernel(x_ref, gkv_ref, wdkv_ref, gckv_ref, gmix_ref, wdq_ref, gq_ref, wq1_ref, wq2_ref,
                wuk_ref, wuv_ref, cos_ref, sin_ref,
                ckv_ref, kr_ref, ckvb_ref, krb_ref, q_ref, *kv_refs, expand):
    x = x_ref[...]
    xn = x * lax.rsqrt(jnp.mean(x * x, axis=-1, keepdims=True) + NORM_EPS)
    cos = cos_ref[...]
    sin = sin_ref[...]
    ckr = _dot((xn * gkv_ref[...]).astype(BF16), wdkv_ref[...])
    ckv = _rms(ckr[:, :KV_LORA], gckv_ref[...])
    ckvb = ckv.astype(BF16)
    ckv_ref[...] = ckv
    ckvb_ref[...] = ckvb
    krot = ckr[:, KV_LORA:KV_LORA + LANES] * cos + ckr[:, KV_LORA + LANES:] * sin
    kr_ref[...] = krot[:, :QK_ROPE]
    krb_ref[...] = krot[:, :QK_ROPE].astype(BF16)
    qscale = MLA_SCALE * LOG2E
    cq = _rms(_dot((xn * gmix_ref[...]).astype(BF16), wdq_ref[...]), gq_ref[...]).astype(BF16)
    y1 = _dot(cq, wq1_ref[...]) * qscale
    y2 = _dot(cq, wq2_ref[...]) * qscale
    for p in range(MLA_HEADS // 2):
        lo = p * PAIR_W
        rot = y1[:, lo + LANES:lo + PAIR_W] * cos + y2[:, p * LANES:(p + 1) * LANES] * sin
        qg = jnp.concatenate([y1[:, lo:lo + LANES], rot], axis=1).astype(BF16)
        if expand:
            q_ref[p] = qg
        else:
            q_ref[:, lo:lo + PAIR_W] = qg
    if expand:
        k_ref, v_ref = kv_refs
        kn = _dot(ckvb, wuk_ref[...])
        vv = _dot(ckvb, wuv_ref[...])
        krb = krot.astype(BF16)
        for p in range(MLA_HEADS // 2):
            k_ref[p] = jnp.concatenate([kn[:, p * LANES:(p + 1) * LANES].astype(BF16), krb], axis=1)
            v_ref[p] = vv[:, p * LANES:(p + 1) * LANES].astype(BF16)


def _kvq(x, w, cos, sin, b, t, tm, expand):
    n = b * t
    nt = cos.shape[0] // tm
    npairs = MLA_HEADS // 2
    tok = lambda d: pl.BlockSpec((tm, d), lambda i: (i, 0))
    tab = pl.BlockSpec((tm, LANES), lambda i: (i % nt, 0))
    res = [w["g_kv_in"], w["wdkv"], w["g_ckv"], w["g_mix1"], w["w_dq"], w["g_q"], w["wq1"], w["wq2"],
           w["wuk_all"], w["wuv_all"]]
    sds = lambda d, dt: jax.ShapeDtypeStruct((n, d), dt)
    out_specs = [tok(KV_LORA), tok(QK_ROPE), tok(KV_LORA), tok(QK_ROPE)]
    out_shape = [sds(KV_LORA, F32), sds(QK_ROPE, F32), sds(KV_LORA, BF16), sds(QK_ROPE, BF16)]
    if expand:
        tpb = t // tm
        pm = lambda d: pl.BlockSpec((None, npairs, tm, d), lambda i: (i // tpb, 0, i % tpb, 0))
        pm_sds = lambda d: jax.ShapeDtypeStruct((b, npairs, t, d), BF16)
        out_specs += [pm(PAIR_W), pm(PAIR_W), pm(LANES)]
        out_shape += [pm_sds(PAIR_W), pm_sds(PAIR_W), pm_sds(LANES)]
    else:
        out_specs += [tok(npairs * PAIR_W)]
        out_shape += [sds(npairs * PAIR_W, BF16)]
    return pl.pallas_call(
        functools.partial(_kvq_kernel, expand=expand),
        grid=(n // tm,),
        in_specs=[tok(D_MODEL)] + [_resident(a.shape) for a in res] + [tab, tab],
        out_specs=out_specs,
        out_shape=out_shape,
        compiler_params=_cparams("parallel"),
        name="kvq_expand" if expand else "kvq",
    )(x, *res, cos, sin)


def _attn_heads_kernel(q_ref, k_ref, v_ref, o_ref, q_s, m_s, l_s, acc_s, sa_s, sb_s):
    tq = q_ref.shape[0]
    kb = k_ref.shape[1]
    rows = 2 * tq
    i = pl.program_id(2)
    q = q_ref[...].astype(F32)
    grp = lax.broadcasted_iota(jnp.int32, q.shape, 1) // QK_ROPE
    zero = jnp.zeros_like(q)
    q_s[0:tq, :] = jnp.where(grp < 2, q, jnp.where(grp == 4, q, zero)).astype(BF16)
    q_s[tq:rows, :] = jnp.where(grp < 2, zero, jnp.where(grp < 4, q, jnp.where(grp == 5, q, zero))).astype(BF16)
    m_s[...] = jnp.full(m_s.shape, NEG_INF, F32)
    l_s[...] = jnp.zeros(l_s.shape, F32)
    acc_s[...] = jnp.zeros(acc_s.shape, F32)

    def scores(j, dst):
        dst[...] = lax.dot_general(q_s[...], k_ref[j], _NT, preferred_element_type=F32)

    def softmax_pv(j, src, diag):
        vblk = v_ref[j]
        for r in range(rows // ATTN_ROWS):
            rs = slice(r * ATTN_ROWS, (r + 1) * ATTN_ROWS)
            s = src[rs, :]
            if diag is not None:
                qc0 = ((r * ATTN_ROWS) % tq) // CHUNK
                qc1 = qc0 + ATTN_ROWS // CHUNK - 1
                kc0 = diag * (kb // CHUNK)
                kc1 = kc0 + kb // CHUNK - 1
                if kc0 > qc1:
                    continue
                if kc1 > qc0:
                    qc = qc0 + lax.broadcasted_iota(jnp.int32, s.shape, 0) // CHUNK
                    kc = kc0 + lax.broadcasted_iota(jnp.int32, s.shape, 1) // CHUNK
                    s = jnp.where(kc <= qc, s, NEG_INF)
            m_prev = m_s[rs, :]
            m_new = jnp.maximum(m_prev, jnp.max(s, axis=1, keepdims=True))
            alpha = jnp.exp2(m_prev - m_new)
            p = jnp.exp2(s - jnp.concatenate([m_new] * (kb // LANES), axis=1))
            psum = p[:, :LANES]
            for c in range(1, kb // LANES):
                psum = psum + p[:, c * LANES:(c + 1) * LANES]
            l_s[rs, :] = alpha * l_s[rs, :] + psum
            acc_s[rs, :] = alpha * acc_s[rs, :] + _dot(p.astype(BF16), vblk)
            m_s[rs, :] = m_new

    nb = tq // kb
    bufs = (sa_s, sb_s)
    scores(0, sa_s)

    def body(jj, carry):
        j = nb * jj
        for d in range(nb):
            scores(j + d + 1, bufs[(d + 1) % 2])
            softmax_pv(j + d, bufs[d % 2], None)
        return carry

    lax.fori_loop(0, i, body, 0)
    j0 = nb * i
    for d in range(nb):
        if d + 1 < nb:
            scores(j0 + d + 1, bufs[(d + 1) % 2])
        softmax_pv(j0 + d, bufs[d % 2], d)

    o = acc_s[...] / jnp.sum(l_s[...], axis=1, keepdims=True)
    lane = lax.broadcasted_iota(jnp.int32, (tq, LANES), 1)
    o_ref[...] = jnp.where(lane < V_HEAD, o[0:tq, :], o[tq:rows, :]).astype(BF16)


def _attn_heads(qcat, kcat, vcat, b, t):
    tq = ATTN_TQ
    assert tq % (2 * KV_BLOCK) == 0 and t % tq == 0
    nq = t // tq
    nkb = t // KV_BLOCK
    npairs = MLA_HEADS // 2
    rows = 2 * tq
    kv = lambda d: pl.BlockSpec((None, None, nkb, KV_BLOCK, d), lambda bi, p, i: (bi, p, 0, 0, 0))
    return pl.pallas_call(
        _attn_heads_kernel,
        grid=(b, npairs, nq),
        in_specs=[pl.BlockSpec((None, None, tq, PAIR_W), lambda bi, p, i: (bi, p, i, 0)),
                  kv(PAIR_W), kv(LANES)],
        out_specs=pl.BlockSpec((tq, LANES), lambda bi, p, i: (bi * nq + i, p)),
        out_shape=jax.ShapeDtypeStruct((b * t, MLA_HEADS * V_HEAD), BF16),
        scratch_shapes=[pltpu.VMEM((rows, PAIR_W), BF16),
                        pltpu.VMEM((rows, LANES), F32), pltpu.VMEM((rows, LANES), F32),
                        pltpu.VMEM((rows, LANES), F32),
                        pltpu.VMEM((rows, KV_BLOCK), F32), pltpu.VMEM((rows, KV_BLOCK), F32)],
        compiler_params=_cparams("parallel", "parallel", "arbitrary"),
        name="attn_heads",
    )(qcat, kcat.reshape(b, npairs, nkb, KV_BLOCK, PAIR_W), vcat.reshape(b, npairs, nkb, KV_BLOCK, LANES))


def _attn_kernel(q_ref, ckv_ref, krt_ref, wuk_ref, wuv_ref, o_ref,
                 q_s, qr_s, m_s, l_s, acc_s, sa_s, sb_s, *, lim):
    tq = q_ref.shape[0]
    kb = ckv_ref.shape[1]
    rows = q_s.shape[0]
    for g in range(MLA_HEADS // 2):
        lo = g * PAIR_W
        ql = _dot(q_ref[:, lo:lo + LANES], wuk_ref[g])
        q_s[(2 * g) * tq:(2 * g + 1) * tq, :] = ql[:, :KV_LORA].astype(BF16)
        q_s[(2 * g + 1) * tq:(2 * g + 2) * tq, :] = ql[:, KV_LORA:].astype(BF16)
        for e in range(2):
            qr_s[(2 * g + e) * tq:(2 * g + e + 1) * tq, :] = (
                q_ref[:, lo + LANES + e * QK_ROPE:lo + LANES + (e + 1) * QK_ROPE])
    m_s[...] = jnp.full(m_s.shape, NEG_INF, F32)
    l_s[...] = jnp.zeros(l_s.shape, F32)
    acc_s[...] = jnp.zeros(acc_s.shape, F32)

    nblk = (lim + kb - 1) // kb

    def scores(j, dst):
        dst[...] = (lax.dot_general(q_s[...], ckv_ref[j], _NT, preferred_element_type=F32)
                    + _dot(qr_s[...], krt_ref[j]))

    def softmax_pv(j, src, masked):
        kblk = ckv_ref[j]
        for r in range(rows // ATTN_ROWS):
            rs = slice(r * ATTN_ROWS, (r + 1) * ATTN_ROWS)
            s = src[rs, :]
            if masked:
                col = lax.broadcasted_iota(jnp.int32, s.shape, 1)
                s = jnp.where(col < lim - j * kb, s, NEG_INF)
            m_prev = m_s[rs, :]
            m_new = jnp.maximum(m_prev, jnp.max(s, axis=1, keepdims=True))
            alpha = jnp.exp2(m_prev - m_new)
            p = jnp.exp2(s - jnp.concatenate([m_new] * (kb // LANES), axis=1))
            psum = p[:, :LANES]
            for c in range(1, kb // LANES):
                psum = psum + p[:, c * LANES:(c + 1) * LANES]
            l_s[rs, :] = alpha * l_s[rs, :] + psum
            acc_s[rs, :] = (jnp.concatenate([alpha] * (KV_LORA // LANES), axis=1) * acc_s[rs, :]
                            + _dot(p.astype(BF16), kblk))
            m_s[rs, :] = m_new

    scores(0, sa_s)
    npairs = (nblk - 1) // 2

    def body(jj, carry):
        j = 2 * jj
        scores(j + 1, sb_s)
        softmax_pv(j, sa_s, False)
        scores(j + 2, sa_s)
        softmax_pv(j + 1, sb_s, False)
        return carry

    lax.fori_loop(0, npairs, body, 0)
    j0 = 2 * npairs
    if nblk - j0 == 2:
        scores(j0 + 1, sb_s)
        softmax_pv(j0, sa_s, False)
        softmax_pv(j0 + 1, sb_s, True)
    else:
        softmax_pv(j0, sa_s, True)

    o_lat = acc_s[...] / jnp.sum(l_s[...], axis=1, keepdims=True)
    for g in range(MLA_HEADS // 2):
        pair = jnp.concatenate([o_lat[(2 * g) * tq:(2 * g + 1) * tq],
                                o_lat[(2 * g + 1) * tq:(2 * g + 2) * tq]], axis=1).astype(BF16)
        o_ref[:, g * LANES:(g + 1) * LANES] = _dot(pair, wuv_ref[g]).astype(BF16)


def _attn(qcat, ckv_blocks, krt_blocks, wuk_pairs, wuv_pairs, b, t, lim):
    tq = CHUNK
    nq = t // tq
    nkb = ckv_blocks.shape[1]
    rows = MLA_HEADS * tq
    qspec = lambda d: pl.BlockSpec((tq, d), lambda bi, i: (bi * nq + i, 0))
    return pl.pallas_call(
        functools.partial(_attn_kernel, lim=lim),
        grid=(b, nq),
        in_specs=[qspec(qcat.shape[1]),
                  pl.BlockSpec((None, nkb, KV_BLOCK, KV_LORA), lambda bi, i: (bi, 0, 0, 0)),
                  pl.BlockSpec((None, nkb, QK_ROPE, KV_BLOCK), lambda bi, i: (bi, 0, 0, 0)),
                  _resident(wuk_pairs.shape), _resident(wuv_pairs.shape)],
        out_specs=qspec(MLA_HEADS * V_HEAD),
        out_shape=jax.ShapeDtypeStruct((b * t, MLA_HEADS * V_HEAD), BF16),
        scratch_shapes=[pltpu.VMEM((rows, KV_LORA), BF16), pltpu.VMEM((rows, QK_ROPE), BF16),
                        pltpu.VMEM((rows, LANES), F32), pltpu.VMEM((rows, LANES), F32),
                        pltpu.VMEM((rows, KV_LORA), F32),
                        pltpu.VMEM((rows, KV_BLOCK), F32), pltpu.VMEM((rows, KV_BLOCK), F32)],
        compiler_params=_cparams("parallel", "arbitrary"),
        name="attn_full",
    )(qcat, ckv_blocks, krt_blocks, wuk_pairs, wuv_pairs)


def _rope_tables(pos, half):
    inv_freq = ROPE_THETA ** (-jnp.arange(half, dtype=F32) / half)
    ang = pos.astype(F32)[:, None] * inv_freq[None, :]
    return jnp.cos(ang), jnp.sin(ang)


def _tile_rows(tab, tm):
    reps = -(-tm // tab.shape[0])
    return jnp.tile(tab, (reps, 1)) if reps > 1 else tab


def _prep_weights(g_mix, g_ffn, w_ret_in, w_ret_out, g_kv_in, w_dkv, g_ckv, w_uk, w_uv,
                  w_dq, g_q, w_uq, w_mla_out, w_ffn_in, w_ffn_out, g_final):
    half = QK_ROPE // 2
    row = lambda g: g.reshape(1, -1)
    npairs = MLA_HEADS // 2
    swap = lambda a: jnp.concatenate([a[..., half:], a[..., :half]], axis=-1)
    w_kr = w_dkv[:, KV_LORA:]
    zk64 = jnp.zeros((D_MODEL, LANES - 2 * QK_ROPE), F32)
    wdkv = jnp.concatenate([w_dkv[:, :KV_LORA], w_kr, w_kr, zk64, swap(w_kr), swap(w_kr), zk64],
                           axis=1).astype(BF16)
    wuq4 = w_uq[0].reshape(Q_LORA, npairs, 2, QK_NOPE + QK_ROPE)
    nope = wuq4[..., :QK_NOPE].reshape(Q_LORA, npairs, 2 * QK_NOPE)
    rope = wuq4[..., QK_NOPE:]
    zq64 = jnp.zeros((Q_LORA, npairs, LANES - 2 * QK_ROPE), F32)
    wq1 = jnp.concatenate([nope, rope.reshape(Q_LORA, npairs, 2 * QK_ROPE), zq64],
                          axis=2).reshape(Q_LORA, npairs * PAIR_W).astype(BF16)
    wq2 = jnp.concatenate([swap(rope).reshape(Q_LORA, npairs, 2 * QK_ROPE), zq64],
                          axis=2).reshape(Q_LORA, npairs * LANES).astype(BF16)
    wuk_t = jnp.transpose(w_uk, (1, 2, 0)).reshape(MLA_HEADS // 2, 2, QK_NOPE, KV_LORA)
    zk = jnp.zeros_like(wuk_t[:, 0])
    wuk_pairs = jnp.concatenate([jnp.concatenate([wuk_t[:, 0], zk], axis=2),
                                 jnp.concatenate([zk, wuk_t[:, 1]], axis=2)], axis=1).astype(BF16)
    wuv_h = jnp.transpose(w_uv, (1, 0, 2)).reshape(MLA_HEADS // 2, 2, KV_LORA, V_HEAD)
    zv = jnp.zeros_like(wuv_h[:, 0])
    wuv_pairs = jnp.concatenate([jnp.concatenate([wuv_h[:, 0], zv], axis=2),
                                 jnp.concatenate([zv, wuv_h[:, 1]], axis=2)], axis=1).astype(BF16)
    return dict(
        g_mix0=row(g_mix[0]), g_mix1=row(g_mix[1]), g_ffn0=row(g_ffn[0]), g_ffn1=row(g_ffn[1]),
        w_ret_in=w_ret_in[0].astype(BF16), w_ret_out=w_ret_out[0].astype(BF16),
        g_kv_in=row(g_kv_in), wdkv=wdkv, g_ckv=row(g_ckv), wuk_pairs=wuk_pairs, wuv_pairs=wuv_pairs,
        w_dq=w_dq[0].astype(BF16), g_q=row(g_q[0]), wq1=wq1, wq2=wq2,
        wuk_all=w_uk.reshape(KV_LORA, -1).astype(BF16), wuv_all=w_uv.reshape(KV_LORA, -1).astype(BF16),
        w_mla_out=w_mla_out[0].astype(BF16),
        w_ffn_in0=w_ffn_in[0].astype(BF16), w_ffn_in1=w_ffn_in[1].astype(BF16),
        w_ffn_out0=w_ffn_out[0].astype(BF16), w_ffn_out1=w_ffn_out[1].astype(BF16),
        g_final=row(g_final))


def _trunk(x, pos, state0, past_ckv, past_krope, w):
    b, t, _ = x.shape
    n = b * t
    is_prompt = past_ckv is None
    xf = x.reshape(n, D_MODEL)

    cos_r, sin_r = _rope_tables(pos, RET_DK // 2)
    tm = min(TM_RET_IN, n)
    q, k, v, sg = _ret_in(xf, w["g_mix0"], w["w_ret_in"], _tile_rows(cos_r, tm), _tile_rows(sin_r, tm), tm)
    z, s_fin = _ret_scan(q, k, v, sg, state0, b, t)
    x1 = _mix_ffn(xf, z, w["w_ret_out"], w["g_ffn0"], w["w_ffn_in0"], w["w_ffn_out0"], w["g_final"],
                  False, min(TM_FFN, n))

    half = QK_ROPE // 2
    inv_freq = ROPE_THETA ** (-jnp.arange(half, dtype=F32) / half)
    npad = LANES - 2 * QK_ROPE
    inv_g = jnp.concatenate([jnp.tile(inv_freq, 4), jnp.zeros((npad,), F32)])
    sign_g = jnp.concatenate([jnp.tile(jnp.repeat(jnp.array([-1.0, 1.0], F32), half), 2),
                              jnp.zeros((npad,), F32)])
    ang_g = pos.astype(F32)[:, None] * inv_g[None, :]
    cos_g = jnp.cos(ang_g)
    sin_g = jnp.sin(ang_g) * sign_g[None, :]
    tm = min(TM_KVQ, n)
    outs = _kvq(x1, w, _tile_rows(cos_g, tm), _tile_rows(sin_g, tm), b, t, tm, is_prompt)
    ckv, kr, ckv_b, kr_b, qcat = outs[:5]

    if is_prompt:
        o = _attn_heads(qcat, outs[5], outs[6], b, t)
    else:
        keys_c = jnp.concatenate([past_ckv.astype(BF16), ckv_b.reshape(b, t, KV_LORA)], axis=1)
        keys_r = jnp.concatenate([past_krope.astype(BF16), kr_b.reshape(b, t, QK_ROPE)], axis=1)
        lim = keys_c.shape[1]
        pad = (-lim) % KV_BLOCK
        if pad:
            keys_c = jnp.pad(keys_c, ((0, 0), (0, pad), (0, 0)))
            keys_r = jnp.pad(keys_r, ((0, 0), (0, pad), (0, 0)))
        nkb = keys_c.shape[1] // KV_BLOCK
        ckv_blocks = keys_c.reshape(b, nkb, KV_BLOCK, KV_LORA)
        krt_blocks = jnp.swapaxes(keys_r.reshape(b, nkb, KV_BLOCK, QK_ROPE), 2, 3)
        o = _attn(qcat, ckv_blocks, krt_blocks, w["wuk_pairs"], w["wuv_pairs"], b, t, lim)

    y = _mix_ffn(x1, o, w["w_mla_out"], w["g_ffn1"], w["w_ffn_in1"], w["w_ffn_out1"], w["g_final"],
                 True, min(TM_FFN, n))
    return (y.reshape(b, t, D_MODEL), s_fin[None], ckv.reshape(b, t, KV_LORA), kr.reshape(b, t, QK_ROPE))


def kernel(x_prompt, x_sample, state_ret, cache_ckv, cache_krope, g_mix, g_ffn, w_ret_in, w_ret_out,
           g_kv_in, w_dkv, g_ckv, w_uk, w_uv, w_dq, g_q, w_uq, w_mla_out, w_ffn_in, w_ffn_out, g_final):
    assert g_mix.shape[0] == 2 and w_ret_in.shape[0] == 1 and w_dq.shape[0] == 1
    assert x_prompt.shape[1] % ATTN_TQ == 0 and x_sample.shape[1] == CHUNK
    w = _prep_weights(g_mix, g_ffn, w_ret_in, w_ret_out, g_kv_in, w_dkv, g_ckv, w_uk, w_uv,
                      w_dq, g_q, w_uq, w_mla_out, w_ffn_in, w_ffn_out, g_final)
    past = cache_ckv.shape[1]
    pos_p = jnp.arange(x_prompt.shape[1], dtype=jnp.int32)
    pos_s = past + jnp.arange(x_sample.shape[1], dtype=jnp.int32)
    zero_state = jnp.zeros((x_prompt.shape[0], RET_HEADS, RET_DK, RET_DV), F32)
    y_p, st_p, ckv_p, kr_p = _trunk(x_prompt, pos_p, zero_state, None, None, w)
    y_s, st_s, ckv_s, kr_s = _trunk(x_sample, pos_s, state_ret[0], cache_ckv, cache_krope, w)
    return (y_p, y_s, st_p.astype(state_ret.dtype), st_s.astype(state_ret.dtype), ckv_p, kr_p, ckv_s, kr_s)
```

```python
import functools
import math

import jax
import jax.numpy as jnp
from jax import lax
from jax.experimental import pallas as pl
from jax.experimental.pallas import tpu as pltpu

F32 = jnp.float32
BF16 = jnp.bfloat16

D_MODEL = 1024
CHUNK = 64
RET_HEADS = 4
RET_DK = D_MODEL // RET_HEADS
RET_DV = 2 * D_MODEL // RET_HEADS
RET_QK = RET_HEADS * RET_DK
RET_V = RET_HEADS * RET_DV
MLA_HEADS = 16
QK_NOPE = 64
QK_ROPE = 32
V_HEAD = 64
Q_LORA = 384
KV_LORA = 256
D_FF = 2816
ROPE_THETA = 10000.0
NORM_EPS = 1e-6
MLA_SCALE = (QK_NOPE + QK_ROPE) ** -0.5
NEG_INF = -1e30

V7X_VMEM_BYTES = 64 * 2**20
VMEM_LIMIT_BYTES = V7X_VMEM_BYTES - 8 * 2**20
LANES = 128
MXU_DIM = 256

RET_CHUNK = MXU_DIM
RET_HEAD_GROUP = 4
KV_BLOCK = MXU_DIM
ATTN_ROWS = 256
ATTN_TQ = 8 * KV_BLOCK
PAIR_W = 2 * LANES
LOG2E = math.log2(math.e)
TM_RET_IN = 512
TM_FFN = 512
TM_KVQ = 256

_NT = (((1,), (1,)), ((), ()))
_TN = (((0,), (0,)), ((), ()))


def _cparams(*sem):
    return pltpu.CompilerParams(dimension_semantics=sem, vmem_limit_bytes=VMEM_LIMIT_BYTES)


def _resident(shape):
    nd = len(shape)
    return pl.BlockSpec(shape, lambda *_: (0,) * nd, pipeline_mode=pl.Buffered(1))


def _rms(x, g):
    return x * lax.rsqrt(jnp.mean(x * x, axis=-1, keepdims=True) + NORM_EPS) * g


def _dot(a, b):
    return jnp.dot(a, b, preferred_element_type=F32)


def _silu(x):
    return x * jax.nn.sigmoid(x)


def _ret_in_kernel(x_ref, g_ref, w_ref, cos_ref, sin_ref, q_ref, k_ref, v_ref, sg_ref):
    hb = _rms(x_ref[...], g_ref[...]).astype(BF16)
    cos = cos_ref[...]
    sin = sin_ref[...]
    half = RET_DK // 2
    for dst, base, scale in ((q_ref, 0, 1.0), (k_ref, RET_QK, RET_DK ** -0.5)):
        for hd in range(RET_HEADS):
            c0 = hd * RET_DK
            y = _dot(hb, w_ref[:, base + c0:base + c0 + RET_DK])
            y1 = y[:, :half]
            y2 = y[:, half:]
            dst[:, c0:c0 + half] = ((y1 * cos - y2 * sin) * scale).astype(BF16)
            dst[:, c0 + half:c0 + RET_DK] = ((y2 * cos + y1 * sin) * scale).astype(BF16)
    v_ref[...] = _dot(hb, w_ref[:, 2 * RET_QK:2 * RET_QK + RET_V]).astype(BF16)
    sg_ref[...] = _silu(_dot(hb, w_ref[:, 2 * RET_QK + RET_V:])).astype(BF16)


def _ret_in(x, g, w, cos, sin, tm):
    n = x.shape[0]
    nt = cos.shape[0] // tm
    tok = lambda d: pl.BlockSpec((tm, d), lambda i: (i, 0))
    tab = pl.BlockSpec((tm, RET_DK // 2), lambda i: (i % nt, 0))
    return pl.pallas_call(
        _ret_in_kernel,
        grid=(n // tm,),
        in_specs=[tok(D_MODEL), _resident(g.shape), _resident(w.shape), tab, tab],
        out_specs=[tok(RET_QK), tok(RET_QK), tok(RET_V), tok(RET_V)],
        out_shape=[jax.ShapeDtypeStruct((n, RET_QK), BF16), jax.ShapeDtypeStruct((n, RET_QK), BF16),
                   jax.ShapeDtypeStruct((n, RET_V), BF16), jax.ShapeDtypeStruct((n, RET_V), BF16)],
        compiler_params=_cparams("parallel"),
        name="ret_in",
    )(x, g, w, cos, sin)


def _ret_scan_kernel(q_ref, k_ref, v_ref, sg_ref, dm_ref, qd_ref, kd_ref, cd_ref, s0_ref,
                     z_ref, sout_ref, s_scr):
    c = pl.program_id(2)

    @pl.when(c == 0)
    def _():
        s_scr[...] = s0_ref[...]

    for h in range(s_scr.shape[0]):
        qs = slice(h * RET_DK, (h + 1) * RET_DK)
        vs = slice(h * RET_DV, (h + 1) * RET_DV)
        q = q_ref[:, qs]
        k = k_ref[:, qs]
        v = v_ref[:, vs]
        s = s_scr[h]
        a = lax.dot_general(q, k, _NT, preferred_element_type=F32) * dm_ref[h]
        y = _dot(a.astype(BF16), v) + qd_ref[h] * _dot(q, s.astype(BF16))
        kd = (k.astype(F32) * kd_ref[h]).astype(BF16)
        s_scr[h] = s * cd_ref[h] + lax.dot_general(kd, v, _TN, preferred_element_type=F32)
        yn = y * lax.rsqrt(jnp.mean(y * y, axis=-1, keepdims=True) + NORM_EPS)
        z_ref[:, vs] = (sg_ref[:, vs].astype(F32) * yn).astype(BF16)

    @pl.when(c == pl.num_programs(2) - 1)
    def _():
        sout_ref[...] = s_scr[...]


def _ret_scan(q, k, v, sg, state0, b, t):
    ck = min(RET_CHUNK, t)
    nc = t // ck
    lg = jnp.log(1.0 - 2.0 ** (-5.0 - jnp.arange(RET_HEADS, dtype=F32)))
    idx = jnp.arange(ck, dtype=F32)
    diff = idx[:, None] - idx[None, :]
    dmask = jnp.where(diff >= 0, jnp.exp(jnp.maximum(diff, 0.0)[None] * lg[:, None, None]), 0.0)
    qdec = jnp.exp((idx + 1.0)[None, :] * lg[:, None])[:, :, None]
    kdec = jnp.exp((ck - 1.0 - idx)[None, :] * lg[:, None])[:, :, None]
    cdec = jnp.broadcast_to(jnp.exp(ck * lg)[:, None, None], (RET_HEADS, 1, RET_DV))

    hg = RET_HEAD_GROUP
    qk_spec = pl.BlockSpec((None, ck, hg * RET_DK), lambda bi, h, c: (bi, c, h))
    v_spec = pl.BlockSpec((None, ck, hg * RET_DV), lambda bi, h, c: (bi, c, h))
    head = lambda r, w: pl.BlockSpec((hg, r, w), lambda bi, h, c: (h, 0, 0))
    st_spec = pl.BlockSpec((None, hg, RET_DK, RET_DV), lambda bi, h, c: (bi, h, 0, 0))
    z, s_out = pl.pallas_call(
        _ret_scan_kernel,
        grid=(b, RET_HEADS // hg, nc),
        in_specs=[qk_spec, qk_spec, v_spec, v_spec, head(ck, ck), head(ck, 1), head(ck, 1),
                  head(1, RET_DV), st_spec],
        out_specs=[v_spec, st_spec],
        out_shape=[jax.ShapeDtypeStruct((b, t, RET_V), BF16),
                   jax.ShapeDtypeStruct((b, RET_HEADS, RET_DK, RET_DV), F32)],
        scratch_shapes=[pltpu.VMEM((hg, RET_DK, RET_DV), F32)],
        compiler_params=_cparams("parallel", "parallel", "arbitrary"),
        name="ret_scan",
    )(q.reshape(b, t, RET_QK), k.reshape(b, t, RET_QK), v.reshape(b, t, RET_V), sg.reshape(b, t, RET_V),
      dmask, qdec, kdec, cdec, state0)
    return z.reshape(b * t, RET_V), s_out


def _mix_ffn_kernel(x_ref, z_ref, wm_ref, g_ref, win_ref, wout_ref, gf_ref, o_ref, *, final):
    x1 = x_ref[...] + _dot(z_ref[...], wm_ref[...])
    hb = _rms(x1, g_ref[...]).astype(BF16)
    gu = _dot(hb, win_ref[...])
    a = (_silu(gu[:, :D_FF]) * gu[:, D_FF:]).astype(BF16)
    x2 = x1 + _dot(a, wout_ref[...])
    if final:
        x2 = _rms(x2, gf_ref[...])
    o_ref[...] = x2


def _mix_ffn(x, z, wm, g, win, wout, gf, final, tm):
    n = x.shape[0]
    tok = lambda d: pl.BlockSpec((tm, d), lambda i: (i, 0))
    return pl.pallas_call(
        functools.partial(_mix_ffn_kernel, final=final),
        grid=(n // tm,),
        in_specs=[tok(D_MODEL), tok(z.shape[1]), _resident(wm.shape), _resident(g.shape),
                  _resident(win.shape), _resident(wout.shape), _resident(gf.shape)],
        out_specs=tok(D_MODEL),
        out_shape=jax.ShapeDtypeStruct((n, D_MODEL), F32),
        compiler_params=_cparams("parallel"),
        name="mix_ffn_final" if final else "mix_ffn",
    )(x, z, wm, g, win, wout, gf)


def _kvq_kernel(x_ref, gkv_ref, wdkv_ref, gckv_ref, gmix_ref, wdq_ref, gq_ref, wq1_ref, wq2_ref,
                wuk_ref, wuv_ref, cos_ref, sin_ref,
                ckv_ref, kr_ref, ckvb_ref, krb_ref, q_ref, *kv_refs, expand):
    x = x_ref[...]
    xn = x * lax.rsqrt(jnp.mean(x * x, axis=-1, keepdims=True) + NORM_EPS)
    cos = cos_ref[...]
    sin = sin_ref[...]
    ckr = _dot((xn * gkv_ref[...]).astype(BF16), wdkv_ref[...])
    ckv = _rms(ckr[:, :KV_LORA], gckv_ref[...])
    ckvb = ckv.astype(BF16)
    ckv_ref[...] = ckv
    ckvb_ref[...] = ckvb
    krot = ckr[:, KV_LORA:KV_LORA + LANES] * cos + ckr[:, KV_LORA + LANES:] * sin
    kr_ref[...] = krot[:, :QK_ROPE]
    krb_ref[...] = krot[:, :QK_ROPE].astype(BF16)
    qscale = MLA_SCALE * LOG2E
    cq = _rms(_dot((xn * gmix_ref[...]).astype(BF16), wdq_ref[...]), gq_ref[...]).astype(BF16)
    y1 = _dot(cq, wq1_ref[...]) * qscale
    y2 = _dot(cq, wq2_ref[...]) * qscale
    for p in range(MLA_HEADS // 2):
        lo = p * PAIR_W
        rot = y1[:, lo + LANES:lo + PAIR_W] * cos + y2[:, p * LANES:(p + 1) * LANES] * sin
        qg = jnp.concatenate([y1[:, lo:lo + LANES], rot], axis=1).astype(BF16)
        if expand:
            q_ref[p] = qg
        else:
            q_ref[:, lo:lo + PAIR_W] = qg
    if expand:
        k_ref, v_ref = kv_refs
        kn = _dot(ckvb, wuk_ref[...])
        vv = _dot(ckvb, wuv_ref[...])
        krb = krot.astype(BF16)
        for p in range(MLA_HEADS // 2):
            k_ref[p] = jnp.concatenate([kn[:, p * LANES:(p + 1) * LANES].astype(BF16), krb], axis=1)
            v_ref[p] = vv[:, p * LANES:(p + 1) * LANES].astype(BF16)


def _kvq(x, w, cos, sin, b, t, tm, expand):
    n = b * t
    nt = cos.shape[0] // tm
    npairs = MLA_HEADS // 2
    tok = lambda d: pl.BlockSpec((tm, d), lambda i: (i, 0))
    tab = pl.BlockSpec((tm, LANES), lambda i: (i % nt, 0))
    res = [w["g_kv_in"], w["wdkv"], w["g_ckv"], w["g_mix1"], w["w_dq"], w["g_q"], w["wq1"], w["wq2"],
           w["wuk_all"], w["wuv_all"]]
    sds = lambda d, dt: jax.ShapeDtypeStruct((n, d), dt)
    out_specs = [tok(KV_LORA), tok(QK_ROPE), tok(KV_LORA), tok(QK_ROPE)]
    out_shape = [sds(KV_LORA, F32), sds(QK_ROPE, F32), sds(KV_LORA, BF16), sds(QK_ROPE, BF16)]
    if expand:
        tpb = t // tm
        pm = lambda d: pl.BlockSpec((None, npairs, tm, d), lambda i: (i // tpb, 0, i % tpb, 0))
        pm_sds = lambda d: jax.ShapeDtypeStruct((b, npairs, t, d), BF16)
        out_specs += [pm(PAIR_W), pm(PAIR_W), pm(LANES)]
        out_shape += [pm_sds(PAIR_W), pm_sds(PAIR_W), pm_sds(LANES)]
    else:
        out_specs += [tok(npairs * PAIR_W)]
        out_shape += [sds(npairs * PAIR_W, BF16)]
    return pl.pallas_call(
        functools.partial(_kvq_kernel, expand=expand),
        grid=(n // tm,),
        in_specs=[tok(D_MODEL)] + [_resident(a.shape) for a in res] + [tab, tab],
        out_specs=out_specs,
        out_shape=out_shape,
        compiler_params=_cparams("parallel"),
        name="kvq_expand" if expand else "kvq",
    )(x, *res, cos, sin)


def _attn_heads_kernel(q_ref, k_ref, v_ref, o_ref, q_s, m_s, l_s, acc_s, sa_s, sb_s):
    tq = q_ref.shape[0]
    kb = k_ref.shape[1]
    rows = 2 * tq
    i = pl.program_id(2)
    q = q_ref[...].astype(F32)
    grp = lax.broadcasted_iota(jnp.int32, q.shape, 1) // QK_ROPE
    zero = jnp.zeros_like(q)
    q_s[0:tq, :] = jnp.where(grp < 2, q, jnp.where(grp == 4, q, zero)).astype(BF16)
    q_s[tq:rows, :] = jnp.where(grp < 2, zero, jnp.where(grp < 4, q, jnp.where(grp == 5, q, zero))).astype(BF16)
    m_s[...] = jnp.full(m_s.shape, NEG_INF, F32)
    l_s[...] = jnp.zeros(l_s.shape, F32)
    acc_s[...] = jnp.zeros(acc_s.shape, F32)

    def scores(j, dst, t0=0):
        if t0 == 0:
            dst[...] = lax.dot_general(q_s[...], k_ref[j], _NT, preferred_element_type=F32)
        else:
            for h in range(2):
                rs = slice(h * tq + t0, (h + 1) * tq)
                dst[rs, :] = lax.dot_general(q_s[rs, :], k_ref[j], _NT, preferred_element_type=F32)

    def softmax_pv(j, src, diag):
        vblk = v_ref[j]
        for r in range(rows // ATTN_ROWS):
            rs = slice(r * ATTN_ROWS, (r + 1) * ATTN_ROWS)
            s = src[rs, :]
            if diag is not None:
                qc0 = ((r * ATTN_ROWS) % tq) // CHUNK
                qc1 = qc0 + ATTN_ROWS // CHUNK - 1
                kc0 = diag * (kb // CHUNK)
                kc1 = kc0 + kb // CHUNK - 1
                if kc0 > qc1:
                    continue
                if kc1 > qc0:
                    qc = qc0 + lax.broadcasted_iota(jnp.int32, s.shape, 0) // CHUNK
                    kc = kc0 + lax.broadcasted_iota(jnp.int32, s.shape, 1) // CHUNK
                    s = jnp.where(kc <= qc, s, NEG_INF)
            m_prev = m_s[rs, :]
            m_new = jnp.maximum(m_prev, jnp.max(s, axis=1, keepdims=True))
            alpha = jnp.exp2(m_prev - m_new)
            p = jnp.exp2(s - jnp.concatenate([m_new] * (kb // LANES), axis=1))
            psum = p[:, :LANES]
            for c in range(1, kb // LANES):
                psum = psum + p[:, c * LANES:(c + 1) * LANES]
            l_s[rs, :] = alpha * l_s[rs, :] + psum
            acc_s[rs, :] = alpha * acc_s[rs, :] + _dot(p.astype(BF16), vblk)
            m_s[rs, :] = m_new

    nb = tq // kb
    bufs = (sa_s, sb_s)
    scores(0, sa_s)

    def body(jj, carry):
        j = nb * jj
        for d in range(nb):
            scores(j + d + 1, bufs[(d + 1) % 2])
            softmax_pv(j + d, bufs[d % 2], None)
        return carry

    lax.fori_loop(0, i, body, 0)
    j0 = nb * i
    for d in range(nb):
        if d + 1 < nb:
            scores(j0 + d + 1, bufs[(d + 1) % 2], (d + 1) * kb)
        softmax_pv(j0 + d, bufs[d % 2], d)

    o = acc_s[...] / jnp.sum(l_s[...], axis=1, keepdims=True)
    lane = lax.broadcasted_iota(jnp.int32, (tq, LANES), 1)
    o_ref[...] = jnp.where(lane < V_HEAD, o[0:tq, :], o[tq:rows, :]).astype(BF16)


def _attn_heads(qcat, kcat, vcat, b, t):
    tq = ATTN_TQ
    assert tq % (2 * KV_BLOCK) == 0 and t % tq == 0
    nq = t // tq
    nkb = t // KV_BLOCK
    npairs = MLA_HEADS // 2
    rows = 2 * tq
    kv = lambda d: pl.BlockSpec((None, None, nkb, KV_BLOCK, d), lambda bi, p, i: (bi, p, 0, 0, 0))
    return pl.pallas_call(
        _attn_heads_kernel,
        grid=(b, npairs, nq),
        in_specs=[pl.BlockSpec((None, None, tq, PAIR_W), lambda bi, p, i: (bi, p, i, 0)),
                  kv(PAIR_W), kv(LANES)],
        out_specs=pl.BlockSpec((tq, LANES), lambda bi, p, i: (bi * nq + i, p)),
        out_shape=jax.ShapeDtypeStruct((b * t, MLA_HEADS * V_HEAD), BF16),
        scratch_shapes=[pltpu.VMEM((rows, PAIR_W), BF16),
                        pltpu.VMEM((rows, LANES), F32), pltpu.VMEM((rows, LANES), F32),
                        pltpu.VMEM((rows, LANES), F32),
                        pltpu.VMEM((rows, KV_BLOCK), F32), pltpu.VMEM((rows, KV_BLOCK), F32)],
        compiler_params=_cparams("parallel", "parallel", "arbitrary"),
        name="attn_heads",
    )(qcat, kcat.reshape(b, npairs, nkb, KV_BLOCK, PAIR_W), vcat.reshape(b, npairs, nkb, KV_BLOCK, LANES))


def _attn_kernel(q_ref, pc_ref, pr_ref, nc_ref, nr_ref, wuk_ref, wuv_ref, o_ref,
                 q_s, qr_s, m_s, l_s, acc_s, sa_s, sb_s):
    tq = q_ref.shape[0]
    nblk = pc_ref.shape[0]
    rows = q_s.shape[0]
    for g in range(MLA_HEADS // 2):
        lo = g * PAIR_W
        ql = _dot(q_ref[:, lo:lo + LANES], wuk_ref[g])
        q_s[(2 * g) * tq:(2 * g + 1) * tq, :] = ql[:, :KV_LORA].astype(BF16)
        q_s[(2 * g + 1) * tq:(2 * g + 2) * tq, :] = ql[:, KV_LORA:].astype(BF16)
        for e in range(2):
            qr_s[(2 * g + e) * tq:(2 * g + e + 1) * tq, :] = (
                q_ref[:, lo + LANES + e * QK_ROPE:lo + LANES + (e + 1) * QK_ROPE])
    m_s[...] = jnp.full(m_s.shape, NEG_INF, F32)
    l_s[...] = jnp.zeros(l_s.shape, F32)
    acc_s[...] = jnp.zeros(acc_s.shape, F32)

    def scores(j, dst):
        dst[...] = (lax.dot_general(q_s[...], pc_ref[j].astype(BF16), _NT, preferred_element_type=F32)
                    + lax.dot_general(qr_s[...], pr_ref[j].astype(BF16), _NT, preferred_element_type=F32))

    def softmax_pv(get_s, vblk):
        nk = vblk.shape[0]
        for r in range(rows // ATTN_ROWS):
            rs = slice(r * ATTN_ROWS, (r + 1) * ATTN_ROWS)
            s = get_s(rs)
            m_prev = m_s[rs, :]
            m_new = jnp.maximum(m_prev, jnp.max(s, axis=1, keepdims=True))
            alpha = jnp.exp2(m_prev - m_new)
            if nk >= LANES:
                p = jnp.exp2(s - jnp.concatenate([m_new] * (nk // LANES), axis=1))
                psum = p[:, :LANES]
                for c in range(1, nk // LANES):
                    psum = psum + p[:, c * LANES:(c + 1) * LANES]
            else:
                p = jnp.exp2(s - m_new[:, :nk])
                psum = jnp.concatenate([p, jnp.zeros((ATTN_ROWS, LANES - nk), F32)], axis=1)
            l_s[rs, :] = alpha * l_s[rs, :] + psum
            acc_s[rs, :] = (jnp.concatenate([alpha] * (KV_LORA // LANES), axis=1) * acc_s[rs, :]
                            + _dot(p.astype(BF16), vblk))
            m_s[rs, :] = m_new

    def past_block(j, src):
        softmax_pv(lambda rs: src[rs, :], pc_ref[j].astype(BF16))

    scores(0, sa_s)
    npairs = (nblk - 1) // 2

    def body(jj, carry):
        j = 2 * jj
        scores(j + 1, sb_s)
        past_block(j, sa_s)
        scores(j + 2, sa_s)
        past_block(j + 1, sb_s)
        return carry

    lax.fori_loop(0, npairs, body, 0)
    j0 = 2 * npairs
    if nblk - j0 == 2:
        scores(j0 + 1, sb_s)
        past_block(j0, sa_s)
        past_block(j0 + 1, sb_s)
    else:
        past_block(j0, sa_s)

    nc = nc_ref[...]
    nr = nr_ref[...]
    softmax_pv(lambda rs: (lax.dot_general(q_s[rs, :], nc, _NT, preferred_element_type=F32)
                           + lax.dot_general(qr_s[rs, :], nr, _NT, preferred_element_type=F32)), nc)

    o_lat = acc_s[...] / jnp.sum(l_s[...], axis=1, keepdims=True)
    for g in range(MLA_HEADS // 2):
        pair = jnp.concatenate([o_lat[(2 * g) * tq:(2 * g + 1) * tq],
                                o_lat[(2 * g + 1) * tq:(2 * g + 2) * tq]], axis=1).astype(BF16)
        o_ref[:, g * LANES:(g + 1) * LANES] = _dot(pair, wuv_ref[g]).astype(BF16)


def _attn(qcat, past_ckv, past_krope, new_ckv, new_krope, wuk_pairs, wuv_pairs, b):
    tq = CHUNK
    past = past_ckv.shape[1]
    assert past % KV_BLOCK == 0 and past >= KV_BLOCK
    nkb = past // KV_BLOCK
    rows = MLA_HEADS * tq
    qspec = lambda d: pl.BlockSpec((tq, d), lambda bi: (bi, 0))
    past_spec = lambda d: pl.BlockSpec((None, nkb, KV_BLOCK, d), lambda bi: (bi, 0, 0, 0))
    return pl.pallas_call(
        _attn_kernel,
        grid=(b,),
        in_specs=[qspec(qcat.shape[1]), past_spec(KV_LORA), past_spec(QK_ROPE),
                  qspec(KV_LORA), qspec(QK_ROPE),
                  _resident(wuk_pairs.shape), _resident(wuv_pairs.shape)],
        out_specs=qspec(MLA_HEADS * V_HEAD),
        out_shape=jax.ShapeDtypeStruct((b * tq, MLA_HEADS * V_HEAD), BF16),
        scratch_shapes=[pltpu.VMEM((rows, KV_LORA), BF16), pltpu.VMEM((rows, QK_ROPE), BF16),
                        pltpu.VMEM((rows, LANES), F32), pltpu.VMEM((rows, LANES), F32),
                        pltpu.VMEM((rows, KV_LORA), F32),
                        pltpu.VMEM((rows, KV_BLOCK), F32), pltpu.VMEM((rows, KV_BLOCK), F32)],
        compiler_params=_cparams("parallel"),
        name="attn_full",
    )(qcat, past_ckv.reshape(b, nkb, KV_BLOCK, KV_LORA), past_krope.reshape(b, nkb, KV_BLOCK, QK_ROPE),
      new_ckv, new_krope, wuk_pairs, wuv_pairs)


def _rope_tables(pos, half):
    inv_freq = ROPE_THETA ** (-jnp.arange(half, dtype=F32) / half)
    ang = pos.astype(F32)[:, None] * inv_freq[None, :]
    return jnp.cos(ang), jnp.sin(ang)


def _tile_rows(tab, tm):
    reps = -(-tm // tab.shape[0])
    return jnp.tile(tab, (reps, 1)) if reps > 1 else tab


def _prep_weights(g_mix, g_ffn, w_ret_in, w_ret_out, g_kv_in, w_dkv, g_ckv, w_uk, w_uv,
                  w_dq, g_q, w_uq, w_mla_out, w_ffn_in, w_ffn_out, g_final):
    half = QK_ROPE // 2
    row = lambda g: g.reshape(1, -1)
    npairs = MLA_HEADS // 2
    swap = lambda a: jnp.concatenate([a[..., half:], a[..., :half]], axis=-1)
    w_kr = w_dkv[:, KV_LORA:]
    zk64 = jnp.zeros((D_MODEL, LANES - 2 * QK_ROPE), F32)
    wdkv = jnp.concatenate([w_dkv[:, :KV_LORA], w_kr, w_kr, zk64, swap(w_kr), swap(w_kr), zk64],
                           axis=1).astype(BF16)
    wuq4 = w_uq[0].reshape(Q_LORA, npairs, 2, QK_NOPE + QK_ROPE)
    nope = wuq4[..., :QK_NOPE].reshape(Q_LORA, npairs, 2 * QK_NOPE)
    rope = wuq4[..., QK_NOPE:]
    zq64 = jnp.zeros((Q_LORA, npairs, LANES - 2 * QK_ROPE), F32)
    wq1 = jnp.concatenate([nope, rope.reshape(Q_LORA, npairs, 2 * QK_ROPE), zq64],
                          axis=2).reshape(Q_LORA, npairs * PAIR_W).astype(BF16)
    wq2 = jnp.concatenate([swap(rope).reshape(Q_LORA, npairs, 2 * QK_ROPE), zq64],
                          axis=2).reshape(Q_LORA, npairs * LANES).astype(BF16)
    wuk_t = jnp.transpose(w_uk, (1, 2, 0)).reshape(MLA_HEADS // 2, 2, QK_NOPE, KV_LORA)
    zk = jnp.zeros_like(wuk_t[:, 0])
    wuk_pairs = jnp.concatenate([jnp.concatenate([wuk_t[:, 0], zk], axis=2),
                                 jnp.concatenate([zk, wuk_t[:, 1]], axis=2)], axis=1).astype(BF16)
    wuv_h = jnp.transpose(w_uv, (1, 0, 2)).reshape(MLA_HEADS // 2, 2, KV_LORA, V_HEAD)
    zv = jnp.zeros_like(wuv_h[:, 0])
    wuv_pairs = jnp.concatenate([jnp.concatenate([wuv_h[:, 0], zv], axis=2),
                                 jnp.concatenate([zv, wuv_h[:, 1]], axis=2)], axis=1).astype(BF16)
    return dict(
        g_mix0=row(g_mix[0]), g_mix1=row(g_mix[1]), g_ffn0=row(g_ffn[0]), g_ffn1=row(g_ffn[1]),
        w_ret_in=w_ret_in[0].astype(BF16), w_ret_out=w_ret_out[0].astype(BF16),
        g_kv_in=row(g_kv_in), wdkv=wdkv, g_ckv=row(g_ckv), wuk_pairs=wuk_pairs, wuv_pairs=wuv_pairs,
        w_dq=w_dq[0].astype(BF16), g_q=row(g_q[0]), wq1=wq1, wq2=wq2,
        wuk_all=w_uk.reshape(KV_LORA, -1).astype(BF16), wuv_all=w_uv.reshape(KV_LORA, -1).astype(BF16),
        w_mla_out=w_mla_out[0].astype(BF16),
        w_ffn_in0=w_ffn_in[0].astype(BF16), w_ffn_in1=w_ffn_in[1].astype(BF16),
        w_ffn_out0=w_ffn_out[0].astype(BF16), w_ffn_out1=w_ffn_out[1].astype(BF16),
        g_final=row(g_final))


def _trunk(x, pos, state0, past_ckv, past_krope, w):
    b, t, _ = x.shape
    n = b * t
    is_prompt = past_ckv is None
    xf = x.reshape(n, D_MODEL)

    cos_r, sin_r = _rope_tables(pos, RET_DK // 2)
    tm = min(TM_RET_IN, n)
    q, k, v, sg = _ret_in(xf, w["g_mix0"], w["w_ret_in"], _tile_rows(cos_r, tm), _tile_rows(sin_r, tm), tm)
    z, s_fin = _ret_scan(q, k, v, sg, state0, b, t)
    x1 = _mix_ffn(xf, z, w["w_ret_out"], w["g_ffn0"], w["w_ffn_in0"], w["w_ffn_out0"], w["g_final"],
                  False, min(TM_FFN, n))

    half = QK_ROPE // 2
    inv_freq = ROPE_THETA ** (-jnp.arange(half, dtype=F32) / half)
    npad = LANES - 2 * QK_ROPE
    inv_g = jnp.concatenate([jnp.tile(inv_freq, 4), jnp.zeros((npad,), F32)])
    sign_g = jnp.concatenate([jnp.tile(jnp.repeat(jnp.array([-1.0, 1.0], F32), half), 2),
                              jnp.zeros((npad,), F32)])
    ang_g = pos.astype(F32)[:, None] * inv_g[None, :]
    cos_g = jnp.cos(ang_g)
    sin_g = jnp.sin(ang_g) * sign_g[None, :]
    tm = min(TM_KVQ, n)
    outs = _kvq(x1, w, _tile_rows(cos_g, tm), _tile_rows(sin_g, tm), b, t, tm, is_prompt)
    ckv, kr, ckv_b, kr_b, qcat = outs[:5]

    if is_prompt:
        o = _attn_heads(qcat, outs[5], outs[6], b, t)
    else:
        o = _attn(qcat, past_ckv, past_krope, ckv_b, kr_b, w["wuk_pairs"], w["wuv_pairs"], b)

    y = _mix_ffn(x1, o, w["w_mla_out"], w["g_ffn1"], w["w_ffn_in1"], w["w_ffn_out1"], w["g_final"],
                 True, min(TM_FFN, n))
    return (y.reshape(b, t, D_MODEL), s_fin[None], ckv.reshape(b, t, KV_LORA), kr.reshape(b, t, QK_ROPE))


def kernel(x_prompt, x_sample, state_ret, cache_ckv, cache_krope, g_mix, g_ffn, w_ret_in, w_ret_out,
           g_kv_in, w_dkv, g_ckv, w_uk, w_uv, w_dq, g_q, w_uq, w_mla_out, w_ffn_in, w_ffn_out, g_final):
    assert g_mix.shape[0] == 2 and w_ret_in.shape[0] == 1 and w_dq.shape[0] == 1
    assert x_prompt.shape[1] % ATTN_TQ == 0 and x_sample.shape[1] == CHUNK
    w = _prep_weights(g_mix, g_ffn, w_ret_in, w_ret_out, g_kv_in, w_dkv, g_ckv, w_uk, w_uv,
                      w_dq, g_q, w_uq, w_mla_out, w_ffn_in, w_ffn_out, g_final)
    past = cache_ckv.shape[1]
    pos_p = jnp.arange(x_prompt.shape[1], dtype=jnp.int32)
    pos_s = past + jnp.arange(x_sample.shape[1], dtype=jnp.int32)
    zero_state = jnp.zeros((x_prompt.shape[0], RET_HEADS, RET_DK, RET_DV), F32)
    y_p, st_p, ckv_p, kr_p = _trunk(x_prompt, pos_p, zero_state, None, None, w)
    y_s, st_s, ckv_s, kr_s = _trunk(x_sample, pos_s, state_ret[0], cache_ckv, cache_krope, w)
    return (y_p, y_s, st_p.astype(state_ret.dtype), st_s.astype(state_ret.dtype), ckv_p, kr_p, ckv_s, kr_s)
```

```python
import functools
import math

import jax
import jax.numpy as jnp
from jax import lax
from jax.experimental import pallas as pl
from jax.experimental.pallas import tpu as pltpu

F32 = jnp.float32
BF16 = jnp.bfloat16

D_MODEL = 1024
CHUNK = 64
RET_HEADS = 4
RET_DK = D_MODEL // RET_HEADS
RET_DV = 2 * D_MODEL // RET_HEADS
RET_QK = RET_HEADS * RET_DK
RET_V = RET_HEADS * RET_DV
MLA_HEADS = 16
QK_NOPE = 64
QK_ROPE = 32
V_HEAD = 64
Q_LORA = 384
KV_LORA = 256
D_FF = 2816
ROPE_THETA = 10000.0
NORM_EPS = 1e-6
MLA_SCALE = (QK_NOPE + QK_ROPE) ** -0.5
NEG_INF = -1e30

V7X_VMEM_BYTES = 64 * 2**20
VMEM_LIMIT_BYTES = V7X_VMEM_BYTES - 8 * 2**20
LANES = 128
MXU_DIM = 256

RET_CHUNK = MXU_DIM
RET_HEAD_GROUP = 4
KV_BLOCK = MXU_DIM
ATTN_ROWS = 256
ATTN_TQ = 8 * KV_BLOCK
PAIR_W = 2 * LANES
LOG2E = math.log2(math.e)
TM_RET_IN = 512
TM_FFN = 512
TM_KVQ = 256

_NT = (((1,), (1,)), ((), ()))
_TN = (((0,), (0,)), ((), ()))


def _cparams(*sem):
    return pltpu.CompilerParams(dimension_semantics=sem, vmem_limit_bytes=VMEM_LIMIT_BYTES)


def _resident(shape):
    nd = len(shape)
    return pl.BlockSpec(shape, lambda *_: (0,) * nd, pipeline_mode=pl.Buffered(1))


def _rms(x, g):
    return x * lax.rsqrt(jnp.mean(x * x, axis=-1, keepdims=True) + NORM_EPS) * g


def _dot(a, b):
    return jnp.dot(a, b, preferred_element_type=F32)


def _silu(x):
    return x * jax.nn.sigmoid(x)


def _ret_in_kernel(x_ref, g_ref, w_ref, cos_ref, sin_ref, q_ref, k_ref, v_ref, sg_ref):
    hb = _rms(x_ref[...], g_ref[...]).astype(BF16)
    cos = cos_ref[...]
    sin = sin_ref[...]
    half = RET_DK // 2
    for dst, base, scale in ((q_ref, 0, 1.0), (k_ref, RET_QK, RET_DK ** -0.5)):
        for hd in range(RET_HEADS):
            c0 = hd * RET_DK
            y = _dot(hb, w_ref[:, base + c0:base + c0 + RET_DK])
            y1 = y[:, :half]
            y2 = y[:, half:]
            dst[:, c0:c0 + half] = ((y1 * cos - y2 * sin) * scale).astype(BF16)
            dst[:, c0 + half:c0 + RET_DK] = ((y2 * cos + y1 * sin) * scale).astype(BF16)
    v_ref[...] = _dot(hb, w_ref[:, 2 * RET_QK:2 * RET_QK + RET_V]).astype(BF16)
    sg_ref[...] = _silu(_dot(hb, w_ref[:, 2 * RET_QK + RET_V:])).astype(BF16)


def _ret_in(x, g, w, cos, sin, tm):
    n = x.shape[0]
    nt = cos.shape[0] // tm
    tok = lambda d: pl.BlockSpec((tm, d), lambda i: (i, 0))
    tab = pl.BlockSpec((tm, RET_DK // 2), lambda i: (i % nt, 0))
    return pl.pallas_call(
        _ret_in_kernel,
        grid=(n // tm,),
        in_specs=[tok(D_MODEL), _resident(g.shape), _resident(w.shape), tab, tab],
        out_specs=[tok(RET_QK), tok(RET_QK), tok(RET_V), tok(RET_V)],
        out_shape=[jax.ShapeDtypeStruct((n, RET_QK), BF16), jax.ShapeDtypeStruct((n, RET_QK), BF16),
                   jax.ShapeDtypeStruct((n, RET_V), BF16), jax.ShapeDtypeStruct((n, RET_V), BF16)],
        compiler_params=_cparams("parallel"),
        name="ret_in",
    )(x, g, w, cos, sin)


def _ret_scan_kernel(q_ref, k_ref, v_ref, sg_ref, dm_ref, qd_ref, kd_ref, cd_ref, s0_ref,
                     z_ref, sout_ref, s_scr):
    c = pl.program_id(2)

    @pl.when(c == 0)
    def _():
        s_scr[...] = s0_ref[...]

    for h in range(s_scr.shape[0]):
        qs = slice(h * RET_DK, (h + 1) * RET_DK)
        vs = slice(h * RET_DV, (h + 1) * RET_DV)
        q = q_ref[:, qs]
        k = k_ref[:, qs]
        v = v_ref[:, vs]
        s = s_scr[h]
        a = lax.dot_general(q, k, _NT, preferred_element_type=F32) * dm_ref[h]
        y = _dot(a.astype(BF16), v) + qd_ref[h] * _dot(q, s.astype(BF16))
        kd = (k.astype(F32) * kd_ref[h]).astype(BF16)
        s_scr[h] = s * cd_ref[h] + lax.dot_general(kd, v, _TN, preferred_element_type=F32)
        yn = y * lax.rsqrt(jnp.mean(y * y, axis=-1, keepdims=True) + NORM_EPS)
        z_ref[:, vs] = (sg_ref[:, vs].astype(F32) * yn).astype(BF16)

    @pl.when(c == pl.num_programs(2) - 1)
    def _():
        sout_ref[...] = s_scr[...]


def _ret_scan(q, k, v, sg, state0, b, t):
    ck = min(RET_CHUNK, t)
    nc = t // ck
    lg = jnp.log(1.0 - 2.0 ** (-5.0 - jnp.arange(RET_HEADS, dtype=F32)))
    idx = jnp.arange(ck, dtype=F32)
    diff = idx[:, None] - idx[None, :]
    dmask = jnp.where(diff >= 0, jnp.exp(jnp.maximum(diff, 0.0)[None] * lg[:, None, None]), 0.0)
    qdec = jnp.exp((idx + 1.0)[None, :] * lg[:, None])[:, :, None]
    kdec = jnp.exp((ck - 1.0 - idx)[None, :] * lg[:, None])[:, :, None]
    cdec = jnp.broadcast_to(jnp.exp(ck * lg)[:, None, None], (RET_HEADS, 1, RET_DV))

    hg = RET_HEAD_GROUP
    qk_spec = pl.BlockSpec((None, ck, hg * RET_DK), lambda bi, h, c: (bi, c, h))
    v_spec = pl.BlockSpec((None, ck, hg * RET_DV), lambda bi, h, c: (bi, c, h))
    head = lambda r, w: pl.BlockSpec((hg, r, w), lambda bi, h, c: (h, 0, 0))
    st_spec = pl.BlockSpec((None, hg, RET_DK, RET_DV), lambda bi, h, c: (bi, h, 0, 0))
    z, s_out = pl.pallas_call(
        _ret_scan_kernel,
        grid=(b, RET_HEADS // hg, nc),
        in_specs=[qk_spec, qk_spec, v_spec, v_spec, head(ck, ck), head(ck, 1), head(ck, 1),
                  head(1, RET_DV), st_spec],
        out_specs=[v_spec, st_spec],
        out_shape=[jax.ShapeDtypeStruct((b, t, RET_V), BF16),
                   jax.ShapeDtypeStruct((b, RET_HEADS, RET_DK, RET_DV), F32)],
        scratch_shapes=[pltpu.VMEM((hg, RET_DK, RET_DV), F32)],
        compiler_params=_cparams("parallel", "parallel", "arbitrary"),
        name="ret_scan",
    )(q.reshape(b, t, RET_QK), k.reshape(b, t, RET_QK), v.reshape(b, t, RET_V), sg.reshape(b, t, RET_V),
      dmask, qdec, kdec, cdec, state0)
    return z.reshape(b * t, RET_V), s_out


def _mix_ffn_kernel(x_ref, z_ref, wm_ref, g_ref, win_ref, wout_ref, gf_ref, o_ref, *, final):
    x1 = x_ref[...] + _dot(z_ref[...], wm_ref[...])
    hb = _rms(x1, g_ref[...]).astype(BF16)
    gu = _dot(hb, win_ref[...])
    a = (_silu(gu[:, :D_FF]) * gu[:, D_FF:]).astype(BF16)
    x2 = x1 + _dot(a, wout_ref[...])
    if final:
        x2 = _rms(x2, gf_ref[...])
    o_ref[...] = x2


def _mix_ffn(x, z, wm, g, win, wout, gf, final, tm):
    n = x.shape[0]
    tok = lambda d: pl.BlockSpec((tm, d), lambda i: (i, 0))
    return pl.pallas_call(
        functools.partial(_mix_ffn_kernel, final=final),
        grid=(n // tm,),
        in_specs=[tok(D_MODEL), tok(z.shape[1]), _resident(wm.shape), _resident(g.shape),
                  _resident(win.shape), _resident(wout.shape), _resident(gf.shape)],
        out_specs=tok(D_MODEL),
        out_shape=jax.ShapeDtypeStruct((n, D_MODEL), F32),
        compiler_params=_cparams("parallel"),
        name="mix_ffn_final" if final else "mix_ffn",
    )(x, z, wm, g, win, wout, gf)


def _kvq_kernel(x_ref, gkv_ref, wdkv_ref, gckv_ref, gmix_ref, wdq_ref, gq_ref, wq1_ref, wq2_ref,
                wuk_ref, wuv_ref, cos_ref, sin_ref,
                ckv_ref, kr_ref, ckvb_ref, krb_ref, q_ref, *kv_refs, expand):
    x = x_ref[...]
    xn = x * lax.rsqrt(jnp.mean(x * x, axis=-1, keepdims=True) + NORM_EPS)
    cos = cos_ref[...]
    sin = sin_ref[...]
    ckr = _dot((xn * gkv_ref[...]).astype(BF16), wdkv_ref[...])
    ckv = _rms(ckr[:, :KV_LORA], gckv_ref[...])
    ckvb = ckv.astype(BF16)
    ckv_ref[...] = ckv
    ckvb_ref[...] = ckvb
    krot = ckr[:, KV_LORA:KV_LORA + LANES] * cos + ckr[:, KV_LORA + LANES:] * sin
    kr_ref[...] = krot[:, :QK_ROPE]
    krb_ref[...] = krot[:, :QK_ROPE].astype(BF16)
    qscale = MLA_SCALE * LOG2E
    cq = _rms(_dot((xn * gmix_ref[...]).astype(BF16), wdq_ref[...]), gq_ref[...]).astype(BF16)
    y1 = _dot(cq, wq1_ref[...]) * qscale
    y2 = _dot(cq, wq2_ref[...]) * qscale
    for p in range(MLA_HEADS // 2):
        lo = p * PAIR_W
        rot = y1[:, lo + LANES:lo + PAIR_W] * cos + y2[:, p * LANES:(p + 1) * LANES] * sin
        qg = jnp.concatenate([y1[:, lo:lo + LANES], rot], axis=1).astype(BF16)
        if expand:
            q_ref[p] = qg
        else:
            q_ref[:, lo:lo + PAIR_W] = qg
    if expand:
        k_ref, v_ref = kv_refs
        kn = _dot(ckvb, wuk_ref[...])
        vv = _dot(ckvb, wuv_ref[...])
        krb = krot.astype(BF16)
        for p in range(MLA_HEADS // 2):
            k_ref[p] = jnp.concatenate([kn[:, p * LANES:(p + 1) * LANES].astype(BF16), krb], axis=1)
            v_ref[p] = vv[:, p * LANES:(p + 1) * LANES].astype(BF16)


def _kvq(x, w, cos, sin, b, t, tm, expand):
    n = b * t
    nt = cos.shape[0] // tm
    npairs = MLA_HEADS // 2
    tok = lambda d: pl.BlockSpec((tm, d), lambda i: (i, 0))
    tab = pl.BlockSpec((tm, LANES), lambda i: (i % nt, 0))
    res = [w["g_kv_in"], w["wdkv"], w["g_ckv"], w["g_mix1"], w["w_dq"], w["g_q"], w["wq1"], w["wq2"],
           w["wuk_all"], w["wuv_all"]]
    sds = lambda d, dt: jax.ShapeDtypeStruct((n, d), dt)
    out_specs = [tok(KV_LORA), tok(QK_ROPE), tok(KV_LORA), tok(QK_ROPE)]
    out_shape = [sds(KV_LORA, F32), sds(QK_ROPE, F32), sds(KV_LORA, BF16), sds(QK_ROPE, BF16)]
    if expand:
        tpb = t // tm
        pm = lambda d: pl.BlockSpec((None, npairs, tm, d), lambda i: (i // tpb, 0, i % tpb, 0))
        pm_sds = lambda d: jax.ShapeDtypeStruct((b, npairs, t, d), BF16)
        out_specs += [pm(PAIR_W), pm(PAIR_W), pm(LANES)]
        out_shape += [pm_sds(PAIR_W), pm_sds(PAIR_W), pm_sds(LANES)]
    else:
        out_specs += [tok(npairs * PAIR_W)]
        out_shape += [sds(npairs * PAIR_W, BF16)]
    return pl.pallas_call(
        functools.partial(_kvq_kernel, expand=expand),
        grid=(n // tm,),
        in_specs=[tok(D_MODEL)] + [_resident(a.shape) for a in res] + [tab, tab],
        out_specs=out_specs,
        out_shape=out_shape,
        compiler_params=_cparams("parallel"),
        name="kvq_expand" if expand else "kvq",
    )(x, *res, cos, sin)


def _attn_heads_kernel(q_ref, k_ref, v_ref, o_ref, q_s, m_s, l_s, acc_s, sa_s, sb_s):
    tq = q_ref.shape[0]
    kb = k_ref.shape[1]
    rows = 2 * tq
    i = pl.program_id(2)
    q = q_ref[...].astype(F32)
    grp = lax.broadcasted_iota(jnp.int32, q.shape, 1) // QK_ROPE
    zero = jnp.zeros_like(q)
    q_s[0:tq, :] = jnp.where(grp < 2, q, jnp.where(grp == 4, q, zero)).astype(BF16)
    q_s[tq:rows, :] = jnp.where(grp < 2, zero, jnp.where(grp < 4, q, jnp.where(grp == 5, q, zero))).astype(BF16)
    m_s[...] = jnp.full(m_s.shape, NEG_INF, F32)
    l_s[...] = jnp.zeros(l_s.shape, F32)
    acc_s[...] = jnp.zeros(acc_s.shape, F32)

    def scores(j, dst, t0=0):
        if t0 == 0:
            dst[...] = lax.dot_general(q_s[...], k_ref[j], _NT, preferred_element_type=F32)
        else:
            for h in range(2):
                rs = slice(h * tq + t0, (h + 1) * tq)
                dst[rs, :] = lax.dot_general(q_s[rs, :], k_ref[j], _NT, preferred_element_type=F32)

    def softmax_pv(j, src, diag):
        vblk = v_ref[j]
        for r in range(rows // ATTN_ROWS):
            rs = slice(r * ATTN_ROWS, (r + 1) * ATTN_ROWS)
            s = src[rs, :]
            if diag is not None:
                qc0 = ((r * ATTN_ROWS) % tq) // CHUNK
                qc1 = qc0 + ATTN_ROWS // CHUNK - 1
                kc0 = diag * (kb // CHUNK)
                kc1 = kc0 + kb // CHUNK - 1
                if kc0 > qc1:
                    continue
                if kc1 > qc0:
                    qc = qc0 + lax.broadcasted_iota(jnp.int32, s.shape, 0) // CHUNK
                    kc = kc0 + lax.broadcasted_iota(jnp.int32, s.shape, 1) // CHUNK
                    s = jnp.where(kc <= qc, s, NEG_INF)
            m_prev = m_s[rs, :]
            m_new = jnp.maximum(m_prev, jnp.max(s, axis=1, keepdims=True))
            alpha = jnp.exp2(m_prev - m_new)
            p = jnp.exp2(s - jnp.concatenate([m_new] * (kb // LANES), axis=1))
            psum = p[:, :LANES]
            for c in range(1, kb // LANES):
                psum = psum + p[:, c * LANES:(c + 1) * LANES]
            l_s[rs, :] = alpha * l_s[rs, :] + psum
            acc_s[rs, :] = alpha * acc_s[rs, :] + _dot(p.astype(BF16), vblk)
            m_s[rs, :] = m_new

    nb = tq // kb
    bufs = (sa_s, sb_s)
    scores(0, sa_s)

    def body(jj, carry):
        j = nb * jj
        for d in range(nb):
            scores(j + d + 1, bufs[(d + 1) % 2])
            softmax_pv(j + d, bufs[d % 2], None)
        return carry

    lax.fori_loop(0, i, body, 0)
    j0 = nb * i
    for d in range(nb):
        if d + 1 < nb:
            scores(j0 + d + 1, bufs[(d + 1) % 2], (d + 1) * kb)
        softmax_pv(j0 + d, bufs[d % 2], d)

    o = acc_s[...] / jnp.sum(l_s[...], axis=1, keepdims=True)
    lane = lax.broadcasted_iota(jnp.int32, (tq, LANES), 1)
    o_ref[...] = jnp.where(lane < V_HEAD, o[0:tq, :], o[tq:rows, :]).astype(BF16)


def _attn_heads(qcat, kcat, vcat, b, t):
    tq = ATTN_TQ
    assert tq % (2 * KV_BLOCK) == 0 and t % tq == 0
    nq = t // tq
    nkb = t // KV_BLOCK
    npairs = MLA_HEADS // 2
    rows = 2 * tq
    kv = lambda d: pl.BlockSpec((None, None, nkb, KV_BLOCK, d), lambda bi, p, i: (bi, p, 0, 0, 0))
    return pl.pallas_call(
        _attn_heads_kernel,
        grid=(b, npairs, nq),
        in_specs=[pl.BlockSpec((None, None, tq, PAIR_W), lambda bi, p, i: (bi, p, i, 0)),
                  kv(PAIR_W), kv(LANES)],
        out_specs=pl.BlockSpec((tq, LANES), lambda bi, p, i: (bi * nq + i, p)),
        out_shape=jax.ShapeDtypeStruct((b * t, MLA_HEADS * V_HEAD), BF16),
        scratch_shapes=[pltpu.VMEM((rows, PAIR_W), BF16),
                        pltpu.VMEM((rows, LANES), F32), pltpu.VMEM((rows, LANES), F32),
                        pltpu.VMEM((rows, LANES), F32),
                        pltpu.VMEM((rows, KV_BLOCK), F32), pltpu.VMEM((rows, KV_BLOCK), F32)],
        compiler_params=_cparams("parallel", "parallel", "arbitrary"),
        name="attn_heads",
    )(qcat, kcat.reshape(b, npairs, nkb, KV_BLOCK, PAIR_W), vcat.reshape(b, npairs, nkb, KV_BLOCK, LANES))


def _attn_kernel(q_ref, pc_ref, pr_ref, nc_ref, nr_ref, wuk_ref, wuv_ref, o_ref,
                 q_s, qr_s, m_s, l_s, acc_s, sa_s, sb_s):
    tq = q_ref.shape[0]
    nblk = pc_ref.shape[0] // KV_BLOCK
    past_rows = lambda j: pl.ds(pl.multiple_of(j * KV_BLOCK, KV_BLOCK), KV_BLOCK)
    rows = q_s.shape[0]
    for g in range(MLA_HEADS // 2):
        lo = g * PAIR_W
        ql = _dot(q_ref[:, lo:lo + LANES], wuk_ref[g])
        q_s[(2 * g) * tq:(2 * g + 1) * tq, :] = ql[:, :KV_LORA].astype(BF16)
        q_s[(2 * g + 1) * tq:(2 * g + 2) * tq, :] = ql[:, KV_LORA:].astype(BF16)
        for e in range(2):
            qr_s[(2 * g + e) * tq:(2 * g + e + 1) * tq, :] = (
                q_ref[:, lo + LANES + e * QK_ROPE:lo + LANES + (e + 1) * QK_ROPE])
    m_s[...] = jnp.full(m_s.shape, NEG_INF, F32)
    l_s[...] = jnp.zeros(l_s.shape, F32)
    acc_s[...] = jnp.zeros(acc_s.shape, F32)

    def scores(j, dst):
        dst[...] = (lax.dot_general(q_s[...], pc_ref[past_rows(j), :].astype(BF16), _NT, preferred_element_type=F32)
                    + lax.dot_general(qr_s[...], pr_ref[past_rows(j), :].astype(BF16), _NT, preferred_element_type=F32))

    def softmax_pv(get_s, vblk):
        nk = vblk.shape[0]
        for r in range(rows // ATTN_ROWS):
            rs = slice(r * ATTN_ROWS, (r + 1) * ATTN_ROWS)
            s = get_s(rs)
            m_prev = m_s[rs, :]
            m_new = jnp.maximum(m_prev, jnp.max(s, axis=1, keepdims=True))
            alpha = jnp.exp2(m_prev - m_new)
            if nk >= LANES:
                p = jnp.exp2(s - jnp.concatenate([m_new] * (nk // LANES), axis=1))
                psum = p[:, :LANES]
                for c in range(1, nk // LANES):
                    psum = psum + p[:, c * LANES:(c + 1) * LANES]
            else:
                p = jnp.exp2(s - m_new[:, :nk])
                psum = jnp.concatenate([p, jnp.zeros((ATTN_ROWS, LANES - nk), F32)], axis=1)
            l_s[rs, :] = alpha * l_s[rs, :] + psum
            acc_s[rs, :] = (jnp.concatenate([alpha] * (KV_LORA // LANES), axis=1) * acc_s[rs, :]
                            + _dot(p.astype(BF16), vblk))
            m_s[rs, :] = m_new

    def past_block(j, src):
        softmax_pv(lambda rs: src[rs, :], pc_ref[past_rows(j), :].astype(BF16))

    scores(0, sa_s)
    npairs = (nblk - 1) // 2

    def body(jj, carry):
        j = 2 * jj
        scores(j + 1, sb_s)
        past_block(j, sa_s)
        scores(j + 2, sa_s)
        past_block(j + 1, sb_s)
        return carry

    lax.fori_loop(0, npairs, body, 0)
    j0 = 2 * npairs
    if nblk - j0 == 2:
        scores(j0 + 1, sb_s)
        past_block(j0, sa_s)
        past_block(j0 + 1, sb_s)
    else:
        past_block(j0, sa_s)

    nc = nc_ref[...]
    nr = nr_ref[...]
    softmax_pv(lambda rs: (lax.dot_general(q_s[rs, :], nc, _NT, preferred_element_type=F32)
                           + lax.dot_general(qr_s[rs, :], nr, _NT, preferred_element_type=F32)), nc)

    o_lat = acc_s[...] / jnp.sum(l_s[...], axis=1, keepdims=True)
    for g in range(MLA_HEADS // 2):
        pair = jnp.concatenate([o_lat[(2 * g) * tq:(2 * g + 1) * tq],
                                o_lat[(2 * g + 1) * tq:(2 * g + 2) * tq]], axis=1).astype(BF16)
        o_ref[:, g * LANES:(g + 1) * LANES] = _dot(pair, wuv_ref[g]).astype(BF16)


def _attn(qcat, past_ckv, past_krope, new_ckv, new_krope, wuk_pairs, wuv_pairs, b):
    tq = CHUNK
    past = past_ckv.shape[1]
    assert past % KV_BLOCK == 0 and past >= KV_BLOCK
    nkb = past // KV_BLOCK
    rows = MLA_HEADS * tq
    qspec = lambda d: pl.BlockSpec((tq, d), lambda bi: (bi, 0))
    past_spec = lambda d: pl.BlockSpec((None, past, d), lambda bi: (bi, 0, 0))
    return pl.pallas_call(
        _attn_kernel,
        grid=(b,),
        in_specs=[qspec(qcat.shape[1]), past_spec(KV_LORA), past_spec(QK_ROPE),
                  qspec(KV_LORA), qspec(QK_ROPE),
                  _resident(wuk_pairs.shape), _resident(wuv_pairs.shape)],
        out_specs=qspec(MLA_HEADS * V_HEAD),
        out_shape=jax.ShapeDtypeStruct((b * tq, MLA_HEADS * V_HEAD), BF16),
        scratch_shapes=[pltpu.VMEM((rows, KV_LORA), BF16), pltpu.VMEM((rows, QK_ROPE), BF16),
                        pltpu.VMEM((rows, LANES), F32), pltpu.VMEM((rows, LANES), F32),
                        pltpu.VMEM((rows, KV_LORA), F32),
                        pltpu.VMEM((rows, KV_BLOCK), F32), pltpu.VMEM((rows, KV_BLOCK), F32)],
        compiler_params=_cparams("parallel"),
        name="attn_full",
    )(qcat, past_ckv, past_krope, new_ckv, new_krope, wuk_pairs, wuv_pairs)


def _rope_tables(pos, half):
    inv_freq = ROPE_THETA ** (-jnp.arange(half, dtype=F32) / half)
    ang = pos.astype(F32)[:, None] * inv_freq[None, :]
    return jnp.cos(ang), jnp.sin(ang)


def _tile_rows(tab, tm):
    reps = -(-tm // tab.shape[0])
    return jnp.tile(tab, (reps, 1)) if reps > 1 else tab


def _prep_weights(g_mix, g_ffn, w_ret_in, w_ret_out, g_kv_in, w_dkv, g_ckv, w_uk, w_uv,
                  w_dq, g_q, w_uq, w_mla_out, w_ffn_in, w_ffn_out, g_final):
    half = QK_ROPE // 2
    row = lambda g: g.reshape(1, -1)
    npairs = MLA_HEADS // 2
    swap = lambda a: jnp.concatenate([a[..., half:], a[..., :half]], axis=-1)
    w_kr = w_dkv[:, KV_LORA:]
    zk64 = jnp.zeros((D_MODEL, LANES - 2 * QK_ROPE), F32)
    wdkv = jnp.concatenate([w_dkv[:, :KV_LORA], w_kr, w_kr, zk64, swap(w_kr), swap(w_kr), zk64],
                           axis=1).astype(BF16)
    wuq4 = w_uq[0].reshape(Q_LORA, npairs, 2, QK_NOPE + QK_ROPE)
    nope = wuq4[..., :QK_NOPE].reshape(Q_LORA, npairs, 2 * QK_NOPE)
    rope = wuq4[..., QK_NOPE:]
    zq64 = jnp.zeros((Q_LORA, npairs, LANES - 2 * QK_ROPE), F32)
    wq1 = jnp.concatenate([nope, rope.reshape(Q_LORA, npairs, 2 * QK_ROPE), zq64],
                          axis=2).reshape(Q_LORA, npairs * PAIR_W).astype(BF16)
    wq2 = jnp.concatenate([swap(rope).reshape(Q_LORA, npairs, 2 * QK_ROPE), zq64],
                          axis=2).reshape(Q_LORA, npairs * LANES).astype(BF16)
    wuk_t = jnp.transpose(w_uk, (1, 2, 0)).reshape(MLA_HEADS // 2, 2, QK_NOPE, KV_LORA)
    zk = jnp.zeros_like(wuk_t[:, 0])
    wuk_pairs = jnp.concatenate([jnp.concatenate([wuk_t[:, 0], zk], axis=2),
                                 jnp.concatenate([zk, wuk_t[:, 1]], axis=2)], axis=1).astype(BF16)
    wuv_h = jnp.transpose(w_uv, (1, 0, 2)).reshape(MLA_HEADS // 2, 2, KV_LORA, V_HEAD)
    zv = jnp.zeros_like(wuv_h[:, 0])
    wuv_pairs = jnp.concatenate([jnp.concatenate([wuv_h[:, 0], zv], axis=2),
                                 jnp.concatenate([zv, wuv_h[:, 1]], axis=2)], axis=1).astype(BF16)
    return dict(
        g_mix0=row(g_mix[0]), g_mix1=row(g_mix[1]), g_ffn0=row(g_ffn[0]), g_ffn1=row(g_ffn[1]),
        w_ret_in=w_ret_in[0].astype(BF16), w_ret_out=w_ret_out[0].astype(BF16),
        g_kv_in=row(g_kv_in), wdkv=wdkv, g_ckv=row(g_ckv), wuk_pairs=wuk_pairs, wuv_pairs=wuv_pairs,
        w_dq=w_dq[0].astype(BF16), g_q=row(g_q[0]), wq1=wq1, wq2=wq2,
        wuk_all=w_uk.reshape(KV_LORA, -1).astype(BF16), wuv_all=w_uv.reshape(KV_LORA, -1).astype(BF16),
        w_mla_out=w_mla_out[0].astype(BF16),
        w_ffn_in0=w_ffn_in[0].astype(BF16), w_ffn_in1=w_ffn_in[1].astype(BF16),
        w_ffn_out0=w_ffn_out[0].astype(BF16), w_ffn_out1=w_ffn_out[1].astype(BF16),
        g_final=row(g_final))


def _trunk(x, pos, state0, past_ckv, past_krope, w):
    b, t, _ = x.shape
    n = b * t
    is_prompt = past_ckv is None
    xf = x.reshape(n, D_MODEL)

    cos_r, sin_r = _rope_tables(pos, RET_DK // 2)
    tm = min(TM_RET_IN, n)
    q, k, v, sg = _ret_in(xf, w["g_mix0"], w["w_ret_in"], _tile_rows(cos_r, tm), _tile_rows(sin_r, tm), tm)
    z, s_fin = _ret_scan(q, k, v, sg, state0, b, t)
    x1 = _mix_ffn(xf, z, w["w_ret_out"], w["g_ffn0"], w["w_ffn_in0"], w["w_ffn_out0"], w["g_final"],
                  False, min(TM_FFN, n))

    half = QK_ROPE // 2
    inv_freq = ROPE_THETA ** (-jnp.arange(half, dtype=F32) / half)
    npad = LANES - 2 * QK_ROPE
    inv_g = jnp.concatenate([jnp.tile(inv_freq, 4), jnp.zeros((npad,), F32)])
    sign_g = jnp.concatenate([jnp.tile(jnp.repeat(jnp.array([-1.0, 1.0], F32), half), 2),
                              jnp.zeros((npad,), F32)])
    ang_g = pos.astype(F32)[:, None] * inv_g[None, :]
    cos_g = jnp.cos(ang_g)
    sin_g = jnp.sin(ang_g) * sign_g[None, :]
    tm = min(TM_KVQ, n)
    outs = _kvq(x1, w, _tile_rows(cos_g, tm), _tile_rows(sin_g, tm), b, t, tm, is_prompt)
    ckv, kr, ckv_b, kr_b, qcat = outs[:5]

    if is_prompt:
        o = _attn_heads(qcat, outs[5], outs[6], b, t)
    else:
        o = _attn(qcat, past_ckv, past_krope, ckv_b, kr_b, w["wuk_pairs"], w["wuv_pairs"], b)

    y = _mix_ffn(x1, o, w["w_mla_out"], w["g_ffn1"], w["w_ffn_in1"], w["w_ffn_out1"], w["g_final"],
                 True, min(TM_FFN, n))
    return (y.reshape(b, t, D_MODEL), s_fin[None], ckv.reshape(b, t, KV_LORA), kr.reshape(b, t, QK_ROPE))


def kernel(x_prompt, x_sample, state_ret, cache_ckv, cache_krope, g_mix, g_ffn, w_ret_in, w_ret_out,
           g_kv_in, w_dkv, g_ckv, w_uk, w_uv, w_dq, g_q, w_uq, w_mla_out, w_ffn_in, w_ffn_out, g_final):
    assert g_mix.shape[0] == 2 and w_ret_in.shape[0] == 1 and w_dq.shape[0] == 1
    assert x_prompt.shape[1] % ATTN_TQ == 0 and x_sample.shape[1] == CHUNK
    w = _prep_weights(g_mix, g_ffn, w_ret_in, w_ret_out, g_kv_in, w_dkv, g_ckv, w_uk, w_uv,
                      w_dq, g_q, w_uq, w_mla_out, w_ffn_in, w_ffn_out, g_final)
    past = cache_ckv.shape[1]
    pos_p = jnp.arange(x_prompt.shape[1], dtype=jnp.int32)
    pos_s = past + jnp.arange(x_sample.shape[1], dtype=jnp.int32)
    zero_state = jnp.zeros((x_prompt.shape[0], RET_HEADS, RET_DK, RET_DV), F32)
    y_p, st_p, ckv_p, kr_p = _trunk(x_prompt, pos_p, zero_state, None, None, w)
    y_s, st_s, ckv_s, kr_s = _trunk(x_sample, pos_s, state_ret[0], cache_ckv, cache_krope, w)
    return (y_p, y_s, st_p.astype(state_ret.dtype), st_s.astype(state_ret.dtype), ckv_p, kr_p, ckv_s, kr_s)
```

```python
import functools
import math

import jax
import jax.numpy as jnp
from jax import lax
from jax.experimental import pallas as pl
from jax.experimental.pallas import tpu as pltpu

F32 = jnp.float32
BF16 = jnp.bfloat16

D_MODEL = 1024
CHUNK = 64
RET_HEADS = 4
RET_DK = D_MODEL // RET_HEADS
RET_DV = 2 * D_MODEL // RET_HEADS
RET_QK = RET_HEADS * RET_DK
RET_V = RET_HEADS * RET_DV
MLA_HEADS = 16
QK_NOPE = 64
QK_ROPE = 32
V_HEAD = 64
Q_LORA = 384
KV_LORA = 256
D_FF = 2816
ROPE_THETA = 10000.0
NORM_EPS = 1e-6
MLA_SCALE = (QK_NOPE + QK_ROPE) ** -0.5
NEG_INF = -1e30

V7X_VMEM_BYTES = 64 * 2**20
VMEM_LIMIT_BYTES = V7X_VMEM_BYTES - 8 * 2**20
LANES = 128
MXU_DIM = 256

RET_CHUNK = MXU_DIM
RET_HEAD_GROUP = 4
KV_BLOCK = MXU_DIM
ATTN_ROWS = 256
ATTN_TQ = 8 * KV_BLOCK
PAIR_W = 2 * LANES
LOG2E = math.log2(math.e)
TM_RET_IN = 512
TM_FFN = 512
TM_KVQ = 256

_NT = (((1,), (1,)), ((), ()))
_TN = (((0,), (0,)), ((), ()))


def _cparams(*sem):
    return pltpu.CompilerParams(dimension_semantics=sem, vmem_limit_bytes=VMEM_LIMIT_BYTES)


def _resident(shape):
    nd = len(shape)
    return pl.BlockSpec(shape, lambda *_: (0,) * nd, pipeline_mode=pl.Buffered(1))


def _rms(x, g):
    return x * lax.rsqrt(jnp.mean(x * x, axis=-1, keepdims=True) + NORM_EPS) * g


def _dot(a, b):
    return jnp.dot(a, b, preferred_element_type=F32)


def _silu(x):
    return x * jax.nn.sigmoid(x)


def _ret_in_kernel(x_ref, g_ref, w_ref, cos_ref, sin_ref, q_ref, k_ref, v_ref, sg_ref):
    hb = _rms(x_ref[...], g_ref[...]).astype(BF16)
    cos = cos_ref[...]
    sin = sin_ref[...]
    half = RET_DK // 2
    for dst, base, scale in ((q_ref, 0, 1.0), (k_ref, RET_QK, RET_DK ** -0.5)):
        for hd in range(RET_HEADS):
            c0 = hd * RET_DK
            y = _dot(hb, w_ref[:, base + c0:base + c0 + RET_DK])
            y1 = y[:, :half]
            y2 = y[:, half:]
            dst[:, c0:c0 + half] = ((y1 * cos - y2 * sin) * scale).astype(BF16)
            dst[:, c0 + half:c0 + RET_DK] = ((y2 * cos + y1 * sin) * scale).astype(BF16)
    v_ref[...] = _dot(hb, w_ref[:, 2 * RET_QK:2 * RET_QK + RET_V]).astype(BF16)
    sg_ref[...] = _silu(_dot(hb, w_ref[:, 2 * RET_QK + RET_V:])).astype(BF16)


def _ret_in(x, g, w, cos, sin, tm):
    n = x.shape[0]
    nt = cos.shape[0] // tm
    tok = lambda d: pl.BlockSpec((tm, d), lambda i: (i, 0))
    tab = pl.BlockSpec((tm, RET_DK // 2), lambda i: (i % nt, 0))
    return pl.pallas_call(
        _ret_in_kernel,
        grid=(n // tm,),
        in_specs=[tok(D_MODEL), _resident(g.shape), _resident(w.shape), tab, tab],
        out_specs=[tok(RET_QK), tok(RET_QK), tok(RET_V), tok(RET_V)],
        out_shape=[jax.ShapeDtypeStruct((n, RET_QK), BF16), jax.ShapeDtypeStruct((n, RET_QK), BF16),
                   jax.ShapeDtypeStruct((n, RET_V), BF16), jax.ShapeDtypeStruct((n, RET_V), BF16)],
        compiler_params=_cparams("parallel"),
        name="ret_in",
    )(x, g, w, cos, sin)


def _ret_scan_kernel(q_ref, k_ref, v_ref, sg_ref, dm_ref, qd_ref, kd_ref, cd_ref, s0_ref,
                     z_ref, sout_ref, s_scr):
    c = pl.program_id(2)

    @pl.when(c == 0)
    def _():
        s_scr[...] = s0_ref[...]

    for h in range(s_scr.shape[0]):
        qs = slice(h * RET_DK, (h + 1) * RET_DK)
        vs = slice(h * RET_DV, (h + 1) * RET_DV)
        q = q_ref[:, qs]
        k = k_ref[:, qs]
        v = v_ref[:, vs]
        s = s_scr[h]
        a = lax.dot_general(q, k, _NT, preferred_element_type=F32) * dm_ref[h]
        y = _dot(a.astype(BF16), v) + qd_ref[h] * _dot(q, s.astype(BF16))
        kd = (k.astype(F32) * kd_ref[h]).astype(BF16)
        s_scr[h] = s * cd_ref[h] + lax.dot_general(kd, v, _TN, preferred_element_type=F32)
        yn = y * lax.rsqrt(jnp.mean(y * y, axis=-1, keepdims=True) + NORM_EPS)
        z_ref[:, vs] = (sg_ref[:, vs].astype(F32) * yn).astype(BF16)

    @pl.when(c == pl.num_programs(2) - 1)
    def _():
        sout_ref[...] = s_scr[...]


def _ret_scan(q, k, v, sg, state0, b, t):
    ck = min(RET_CHUNK, t)
    nc = t // ck
    lg = jnp.log(1.0 - 2.0 ** (-5.0 - jnp.arange(RET_HEADS, dtype=F32)))
    idx = jnp.arange(ck, dtype=F32)
    diff = idx[:, None] - idx[None, :]
    dmask = jnp.where(diff >= 0, jnp.exp(jnp.maximum(diff, 0.0)[None] * lg[:, None, None]), 0.0)
    qdec = jnp.exp((idx + 1.0)[None, :] * lg[:, None])[:, :, None]
    kdec = jnp.exp((ck - 1.0 - idx)[None, :] * lg[:, None])[:, :, None]
    cdec = jnp.broadcast_to(jnp.exp(ck * lg)[:, None, None], (RET_HEADS, 1, RET_DV))

    hg = RET_HEAD_GROUP
    qk_spec = pl.BlockSpec((None, ck, hg * RET_DK), lambda bi, h, c: (bi, c, h))
    v_spec = pl.BlockSpec((None, ck, hg * RET_DV), lambda bi, h, c: (bi, c, h))
    head = lambda r, w: pl.BlockSpec((hg, r, w), lambda bi, h, c: (h, 0, 0))
    st_spec = pl.BlockSpec((None, hg, RET_DK, RET_DV), lambda bi, h, c: (bi, h, 0, 0))
    z, s_out = pl.pallas_call(
        _ret_scan_kernel,
        grid=(b, RET_HEADS // hg, nc),
        in_specs=[qk_spec, qk_spec, v_spec, v_spec, head(ck, ck), head(ck, 1), head(ck, 1),
                  head(1, RET_DV), st_spec],
        out_specs=[v_spec, st_spec],
        out_shape=[jax.ShapeDtypeStruct((b, t, RET_V), BF16),
                   jax.ShapeDtypeStruct((b, RET_HEADS, RET_DK, RET_DV), F32)],
        scratch_shapes=[pltpu.VMEM((hg, RET_DK, RET_DV), F32)],
        compiler_params=_cparams("parallel", "parallel", "arbitrary"),
        name="ret_scan",
    )(q.reshape(b, t, RET_QK), k.reshape(b, t, RET_QK), v.reshape(b, t, RET_V), sg.reshape(b, t, RET_V),
      dmask, qdec, kdec, cdec, state0)
    return z.reshape(b * t, RET_V), s_out


def _mix_ffn_kernel(x_ref, z_ref, wm_ref, g_ref, win_ref, wout_ref, gf_ref, o_ref, *, final):
    x1 = x_ref[...] + _dot(z_ref[...], wm_ref[...])
    hb = _rms(x1, g_ref[...]).astype(BF16)
    gu = _dot(hb, win_ref[...])
    a = (_silu(gu[:, :D_FF]) * gu[:, D_FF:]).astype(BF16)
    x2 = x1 + _dot(a, wout_ref[...])
    if final:
        x2 = _rms(x2, gf_ref[...])
    o_ref[...] = x2


def _mix_ffn(x, z, wm, g, win, wout, gf, final, tm):
    n = x.shape[0]
    tok = lambda d: pl.BlockSpec((tm, d), lambda i: (i, 0))
    return pl.pallas_call(
        functools.partial(_mix_ffn_kernel, final=final),
        grid=(n // tm,),
        in_specs=[tok(D_MODEL), tok(z.shape[1]), _resident(wm.shape), _resident(g.shape),
                  _resident(win.shape), _resident(wout.shape), _resident(gf.shape)],
        out_specs=tok(D_MODEL),
        out_shape=jax.ShapeDtypeStruct((n, D_MODEL), F32),
        compiler_params=_cparams("parallel"),
        name="mix_ffn_final" if final else "mix_ffn",
    )(x, z, wm, g, win, wout, gf)


def _kvq_kernel(x_ref, gkv_ref, wdkv_ref, gckv_ref, gmix_ref, wdq_ref, gq_ref, wq1_ref, wq2_ref,
                wuk_ref, wuv_ref, cos_ref, sin_ref,
                ckv_ref, kr_ref, *out_refs, expand):
    if expand:
        q_ref, k_ref, v_ref = out_refs
    else:
        ckvb_ref, krb_ref, q_ref = out_refs
    x = x_ref[...]
    xn = x * lax.rsqrt(jnp.mean(x * x, axis=-1, keepdims=True) + NORM_EPS)
    cos = cos_ref[...]
    sin = sin_ref[...]
    ckr = _dot((xn * gkv_ref[...]).astype(BF16), wdkv_ref[...])
    ckv = _rms(ckr[:, :KV_LORA], gckv_ref[...])
    ckvb = ckv.astype(BF16)
    ckv_ref[...] = ckv
    krot = ckr[:, KV_LORA:KV_LORA + LANES] * cos + ckr[:, KV_LORA + LANES:] * sin
    kr_ref[...] = krot[:, :QK_ROPE]
    if not expand:
        ckvb_ref[...] = ckvb
        krb_ref[...] = krot[:, :QK_ROPE].astype(BF16)
    qscale = MLA_SCALE * LOG2E
    cq = _rms(_dot((xn * gmix_ref[...]).astype(BF16), wdq_ref[...]), gq_ref[...]).astype(BF16)
    y1 = _dot(cq, wq1_ref[...]) * qscale
    y2 = _dot(cq, wq2_ref[...]) * qscale
    for p in range(MLA_HEADS // 2):
        lo = p * PAIR_W
        rot = y1[:, lo + LANES:lo + PAIR_W] * cos + y2[:, p * LANES:(p + 1) * LANES] * sin
        qg = jnp.concatenate([y1[:, lo:lo + LANES], rot], axis=1).astype(BF16)
        if expand:
            q_ref[p] = qg
        else:
            q_ref[:, lo:lo + PAIR_W] = qg
    if expand:
        kn = _dot(ckvb, wuk_ref[...])
        vv = _dot(ckvb, wuv_ref[...])
        krb = krot.astype(BF16)
        for p in range(MLA_HEADS // 2):
            k_ref[p] = jnp.concatenate([kn[:, p * LANES:(p + 1) * LANES].astype(BF16), krb], axis=1)
            v_ref[p] = vv[:, p * LANES:(p + 1) * LANES].astype(BF16)


def _kvq(x, w, cos, sin, b, t, tm, expand):
    n = b * t
    nt = cos.shape[0] // tm
    npairs = MLA_HEADS // 2
    tok = lambda d: pl.BlockSpec((tm, d), lambda i: (i, 0))
    tab = pl.BlockSpec((tm, LANES), lambda i: (i % nt, 0))
    res = [w["g_kv_in"], w["wdkv"], w["g_ckv"], w["g_mix1"], w["w_dq"], w["g_q"], w["wq1"], w["wq2"],
           w["wuk_all"], w["wuv_all"]]
    sds = lambda d, dt: jax.ShapeDtypeStruct((n, d), dt)
    out_specs = [tok(KV_LORA), tok(QK_ROPE)]
    out_shape = [sds(KV_LORA, F32), sds(QK_ROPE, F32)]
    if expand:
        tpb = t // tm
        pm = lambda d: pl.BlockSpec((None, npairs, tm, d), lambda i: (i // tpb, 0, i % tpb, 0))
        pm_sds = lambda d: jax.ShapeDtypeStruct((b, npairs, t, d), BF16)
        out_specs += [pm(PAIR_W), pm(PAIR_W), pm(LANES)]
        out_shape += [pm_sds(PAIR_W), pm_sds(PAIR_W), pm_sds(LANES)]
    else:
        out_specs += [tok(KV_LORA), tok(QK_ROPE), tok(npairs * PAIR_W)]
        out_shape += [sds(KV_LORA, BF16), sds(QK_ROPE, BF16), sds(npairs * PAIR_W, BF16)]
    return pl.pallas_call(
        functools.partial(_kvq_kernel, expand=expand),
        grid=(n // tm,),
        in_specs=[tok(D_MODEL)] + [_resident(a.shape) for a in res] + [tab, tab],
        out_specs=out_specs,
        out_shape=out_shape,
        compiler_params=_cparams("parallel"),
        name="kvq_expand" if expand else "kvq",
    )(x, *res, cos, sin)


def _attn_heads_kernel(q_ref, k_ref, v_ref, o_ref, q_s, m_s, l_s, acc_s, sa_s, sb_s):
    tq = q_ref.shape[0]
    kb = k_ref.shape[1]
    rows = 2 * tq
    i = pl.program_id(2)
    q = q_ref[...].astype(F32)
    grp = lax.broadcasted_iota(jnp.int32, q.shape, 1) // QK_ROPE
    zero = jnp.zeros_like(q)
    q_s[0:tq, :] = jnp.where(grp < 2, q, jnp.where(grp == 4, q, zero)).astype(BF16)
    q_s[tq:rows, :] = jnp.where(grp < 2, zero, jnp.where(grp < 4, q, jnp.where(grp == 5, q, zero))).astype(BF16)
    m_s[...] = jnp.full(m_s.shape, NEG_INF, F32)
    l_s[...] = jnp.zeros(l_s.shape, F32)
    acc_s[...] = jnp.zeros(acc_s.shape, F32)

    def scores(j, dst, t0=0):
        if t0 == 0:
            dst[...] = lax.dot_general(q_s[...], k_ref[j], _NT, preferred_element_type=F32)
        else:
            for h in range(2):
                rs = slice(h * tq + t0, (h + 1) * tq)
                dst[rs, :] = lax.dot_general(q_s[rs, :], k_ref[j], _NT, preferred_element_type=F32)

    def softmax_pv(j, src, diag):
        vblk = v_ref[j]
        for r in range(rows // ATTN_ROWS):
            rs = slice(r * ATTN_ROWS, (r + 1) * ATTN_ROWS)
            s = src[rs, :]
            if diag is not None:
                qc0 = ((r * ATTN_ROWS) % tq) // CHUNK
                qc1 = qc0 + ATTN_ROWS // CHUNK - 1
                kc0 = diag * (kb // CHUNK)
                kc1 = kc0 + kb // CHUNK - 1
                if kc0 > qc1:
                    continue
                if kc1 > qc0:
                    qc = qc0 + lax.broadcasted_iota(jnp.int32, s.shape, 0) // CHUNK
                    kc = kc0 + lax.broadcasted_iota(jnp.int32, s.shape, 1) // CHUNK
                    s = jnp.where(kc <= qc, s, NEG_INF)
            m_prev = m_s[rs, :]
            m_new = jnp.maximum(m_prev, jnp.max(s, axis=1, keepdims=True))
            alpha = jnp.exp2(m_prev - m_new)
            p = jnp.exp2(s - jnp.concatenate([m_new] * (kb // LANES), axis=1))
            psum = p[:, :LANES]
            for c in range(1, kb // LANES):
                psum = psum + p[:, c * LANES:(c + 1) * LANES]
            l_s[rs, :] = alpha * l_s[rs, :] + psum
            acc_s[rs, :] = alpha * acc_s[rs, :] + _dot(p.astype(BF16), vblk)
            m_s[rs, :] = m_new

    nb = tq // kb
    bufs = (sa_s, sb_s)
    scores(0, sa_s)

    def body(jj, carry):
        j = nb * jj
        for d in range(nb):
            scores(j + d + 1, bufs[(d + 1) % 2])
            softmax_pv(j + d, bufs[d % 2], None)
        return carry

    lax.fori_loop(0, i, body, 0)
    j0 = nb * i
    for d in range(nb):
        if d + 1 < nb:
            scores(j0 + d + 1, bufs[(d + 1) % 2], (d + 1) * kb)
        softmax_pv(j0 + d, bufs[d % 2], d)

    o = acc_s[...] / jnp.sum(l_s[...], axis=1, keepdims=True)
    lane = lax.broadcasted_iota(jnp.int32, (tq, LANES), 1)
    o_ref[...] = jnp.where(lane < V_HEAD, o[0:tq, :], o[tq:rows, :]).astype(BF16)


def _attn_heads(qcat, kcat, vcat, b, t):
    tq = ATTN_TQ
    assert tq % (2 * KV_BLOCK) == 0 and t % tq == 0
    nq = t // tq
    nkb = t // KV_BLOCK
    npairs = MLA_HEADS // 2
    rows = 2 * tq
    kv = lambda d: pl.BlockSpec((None, None, nkb, KV_BLOCK, d), lambda bi, p, i: (bi, p, 0, 0, 0))
    return pl.pallas_call(
        _attn_heads_kernel,
        grid=(b, npairs, nq),
        in_specs=[pl.BlockSpec((None, None, tq, PAIR_W), lambda bi, p, i: (bi, p, i, 0)),
                  kv(PAIR_W), kv(LANES)],
        out_specs=pl.BlockSpec((tq, LANES), lambda bi, p, i: (bi * nq + i, p)),
        out_shape=jax.ShapeDtypeStruct((b * t, MLA_HEADS * V_HEAD), BF16),
        scratch_shapes=[pltpu.VMEM((rows, PAIR_W), BF16),
                        pltpu.VMEM((rows, LANES), F32), pltpu.VMEM((rows, LANES), F32),
                        pltpu.VMEM((rows, LANES), F32),
                        pltpu.VMEM((rows, KV_BLOCK), F32), pltpu.VMEM((rows, KV_BLOCK), F32)],
        compiler_params=_cparams("parallel", "parallel", "arbitrary"),
        name="attn_heads",
    )(qcat, kcat.reshape(b, npairs, nkb, KV_BLOCK, PAIR_W), vcat.reshape(b, npairs, nkb, KV_BLOCK, LANES))


def _attn_kernel(q_ref, pc_ref, pr_ref, nc_ref, nr_ref, wuk_ref, wuv_ref, o_ref,
                 q_s, qr_s, m_s, l_s, acc_s, sa_s, sb_s):
    tq = q_ref.shape[0]
    nblk = pc_ref.shape[0] // KV_BLOCK
    past_rows = lambda j: pl.ds(pl.multiple_of(j * KV_BLOCK, KV_BLOCK), KV_BLOCK)
    rows = q_s.shape[0]
    for g in range(MLA_HEADS // 2):
        lo = g * PAIR_W
        ql = _dot(q_ref[:, lo:lo + LANES], wuk_ref[g])
        q_s[(2 * g) * tq:(2 * g + 1) * tq, :] = ql[:, :KV_LORA].astype(BF16)
        q_s[(2 * g + 1) * tq:(2 * g + 2) * tq, :] = ql[:, KV_LORA:].astype(BF16)
        for e in range(2):
            qr_s[(2 * g + e) * tq:(2 * g + e + 1) * tq, :] = (
                q_ref[:, lo + LANES + e * QK_ROPE:lo + LANES + (e + 1) * QK_ROPE])
    m_s[...] = jnp.full(m_s.shape, NEG_INF, F32)
    l_s[...] = jnp.zeros(l_s.shape, F32)
    acc_s[...] = jnp.zeros(acc_s.shape, F32)

    def scores(j, dst):
        dst[...] = (lax.dot_general(q_s[...], pc_ref[past_rows(j), :].astype(BF16), _NT, preferred_element_type=F32)
                    + lax.dot_general(qr_s[...], pr_ref[past_rows(j), :].astype(BF16), _NT, preferred_element_type=F32))

    def softmax_pv(get_s, vblk):
        nk = vblk.shape[0]
        for r in range(rows // ATTN_ROWS):
            rs = slice(r * ATTN_ROWS, (r + 1) * ATTN_ROWS)
            s = get_s(rs)
            m_prev = m_s[rs, :]
            m_new = jnp.maximum(m_prev, jnp.max(s, axis=1, keepdims=True))
            alpha = jnp.exp2(m_prev - m_new)
            if nk >= LANES:
                p = jnp.exp2(s - jnp.concatenate([m_new] * (nk // LANES), axis=1))
                psum = p[:, :LANES]
                for c in range(1, nk // LANES):
                    psum = psum + p[:, c * LANES:(c + 1) * LANES]
            else:
                p = jnp.exp2(s - m_new[:, :nk])
                psum = jnp.concatenate([p, jnp.zeros((ATTN_ROWS, LANES - nk), F32)], axis=1)
            l_s[rs, :] = alpha * l_s[rs, :] + psum
            acc_s[rs, :] = (jnp.concatenate([alpha] * (KV_LORA // LANES), axis=1) * acc_s[rs, :]
                            + _dot(p.astype(BF16), vblk))
            m_s[rs, :] = m_new

    def past_block(j, src):
        softmax_pv(lambda rs: src[rs, :], pc_ref[past_rows(j), :].astype(BF16))

    bufs = (sa_s, sb_s)
    per_iter = 4
    niter = (nblk - 1) // per_iter
    scores(0, sa_s)

    def body(jj, carry):
        j = per_iter * jj
        for d in range(per_iter):
            scores(j + d + 1, bufs[(d + 1) % 2])
            past_block(j + d, bufs[d % 2])
        return carry

    lax.fori_loop(0, niter, body, 0)
    j0 = per_iter * niter
    for d in range(nblk - j0):
        if j0 + d + 1 < nblk:
            scores(j0 + d + 1, bufs[(d + 1) % 2])
        past_block(j0 + d, bufs[d % 2])

    nc = nc_ref[...]
    nr = nr_ref[...]
    softmax_pv(lambda rs: (lax.dot_general(q_s[rs, :], nc, _NT, preferred_element_type=F32)
                           + lax.dot_general(qr_s[rs, :], nr, _NT, preferred_element_type=F32)), nc)

    o_lat = acc_s[...] / jnp.sum(l_s[...], axis=1, keepdims=True)
    for g in range(MLA_HEADS // 2):
        pair = jnp.concatenate([o_lat[(2 * g) * tq:(2 * g + 1) * tq],
                                o_lat[(2 * g + 1) * tq:(2 * g + 2) * tq]], axis=1).astype(BF16)
        o_ref[:, g * LANES:(g + 1) * LANES] = _dot(pair, wuv_ref[g]).astype(BF16)


def _attn(qcat, past_ckv, past_krope, new_ckv, new_krope, wuk_pairs, wuv_pairs, b):
    tq = CHUNK
    past = past_ckv.shape[1]
    assert past % KV_BLOCK == 0 and past >= KV_BLOCK
    rows = MLA_HEADS * tq
    qspec = lambda d: pl.BlockSpec((tq, d), lambda bi: (bi, 0))
    past_spec = lambda d: pl.BlockSpec((None, past, d), lambda bi: (bi, 0, 0))
    return pl.pallas_call(
        _attn_kernel,
        grid=(b,),
        in_specs=[qspec(qcat.shape[1]), past_spec(KV_LORA), past_spec(QK_ROPE),
                  qspec(KV_LORA), qspec(QK_ROPE),
                  _resident(wuk_pairs.shape), _resident(wuv_pairs.shape)],
        out_specs=qspec(MLA_HEADS * V_HEAD),
        out_shape=jax.ShapeDtypeStruct((b * tq, MLA_HEADS * V_HEAD), BF16),
        scratch_shapes=[pltpu.VMEM((rows, KV_LORA), BF16), pltpu.VMEM((rows, QK_ROPE), BF16),
                        pltpu.VMEM((rows, LANES), F32), pltpu.VMEM((rows, LANES), F32),
                        pltpu.VMEM((rows, KV_LORA), F32),
                        pltpu.VMEM((rows, KV_BLOCK), F32), pltpu.VMEM((rows, KV_BLOCK), F32)],
        compiler_params=_cparams("parallel"),
        name="attn_full",
    )(qcat, past_ckv, past_krope, new_ckv, new_krope, wuk_pairs, wuv_pairs)


def _rope_tables(pos, half):
    inv_freq = ROPE_THETA ** (-jnp.arange(half, dtype=F32) / half)
    ang = pos.astype(F32)[:, None] * inv_freq[None, :]
    return jnp.cos(ang), jnp.sin(ang)


def _tile_rows(tab, tm):
    reps = -(-tm // tab.shape[0])
    return jnp.tile(tab, (reps, 1)) if reps > 1 else tab


def _prep_weights(g_mix, g_ffn, w_ret_in, w_ret_out, g_kv_in, w_dkv, g_ckv, w_uk, w_uv,
                  w_dq, g_q, w_uq, w_mla_out, w_ffn_in, w_ffn_out, g_final):
    half = QK_ROPE // 2
    row = lambda g: g.reshape(1, -1)
    npairs = MLA_HEADS // 2
    swap = lambda a: jnp.concatenate([a[..., half:], a[..., :half]], axis=-1)
    w_kr = w_dkv[:, KV_LORA:]
    zk64 = jnp.zeros((D_MODEL, LANES - 2 * QK_ROPE), F32)
    wdkv = jnp.concatenate([w_dkv[:, :KV_LORA], w_kr, w_kr, zk64, swap(w_kr), swap(w_kr), zk64],
                           axis=1).astype(BF16)
    wuq4 = w_uq[0].reshape(Q_LORA, npairs, 2, QK_NOPE + QK_ROPE)
    nope = wuq4[..., :QK_NOPE].reshape(Q_LORA, npairs, 2 * QK_NOPE)
    rope = wuq4[..., QK_NOPE:]
    zq64 = jnp.zeros((Q_LORA, npairs, LANES - 2 * QK_ROPE), F32)
    wq1 = jnp.concatenate([nope, rope.reshape(Q_LORA, npairs, 2 * QK_ROPE), zq64],
                          axis=2).reshape(Q_LORA, npairs * PAIR_W).astype(BF16)
    wq2 = jnp.concatenate([swap(rope).reshape(Q_LORA, npairs, 2 * QK_ROPE), zq64],
                          axis=2).reshape(Q_LORA, npairs * LANES).astype(BF16)
    wuk_t = jnp.transpose(w_uk, (1, 2, 0)).reshape(MLA_HEADS // 2, 2, QK_NOPE, KV_LORA)
    zk = jnp.zeros_like(wuk_t[:, 0])
    wuk_pairs = jnp.concatenate([jnp.concatenate([wuk_t[:, 0], zk], axis=2),
                                 jnp.concatenate([zk, wuk_t[:, 1]], axis=2)], axis=1).astype(BF16)
    wuv_h = jnp.transpose(w_uv, (1, 0, 2)).reshape(MLA_HEADS // 2, 2, KV_LORA, V_HEAD)
    zv = jnp.zeros_like(wuv_h[:, 0])
    wuv_pairs = jnp.concatenate([jnp.concatenate([wuv_h[:, 0], zv], axis=2),
                                 jnp.concatenate([zv, wuv_h[:, 1]], axis=2)], axis=1).astype(BF16)
    return dict(
        g_mix0=row(g_mix[0]), g_mix1=row(g_mix[1]), g_ffn0=row(g_ffn[0]), g_ffn1=row(g_ffn[1]),
        w_ret_in=w_ret_in[0].astype(BF16), w_ret_out=w_ret_out[0].astype(BF16),
        g_kv_in=row(g_kv_in), wdkv=wdkv, g_ckv=row(g_ckv), wuk_pairs=wuk_pairs, wuv_pairs=wuv_pairs,
        w_dq=w_dq[0].astype(BF16), g_q=row(g_q[0]), wq1=wq1, wq2=wq2,
        wuk_all=w_uk.reshape(KV_LORA, -1).astype(BF16), wuv_all=w_uv.reshape(KV_LORA, -1).astype(BF16),
        w_mla_out=w_mla_out[0].astype(BF16),
        w_ffn_in0=w_ffn_in[0].astype(BF16), w_ffn_in1=w_ffn_in[1].astype(BF16),
        w_ffn_out0=w_ffn_out[0].astype(BF16), w_ffn_out1=w_ffn_out[1].astype(BF16),
        g_final=row(g_final))


def _trunk(x, pos, state0, past_ckv, past_krope, w):
    b, t, _ = x.shape
    n = b * t
    is_prompt = past_ckv is None
    xf = x.reshape(n, D_MODEL)

    cos_r, sin_r = _rope_tables(pos, RET_DK // 2)
    tm = min(TM_RET_IN, n)
    q, k, v, sg = _ret_in(xf, w["g_mix0"], w["w_ret_in"], _tile_rows(cos_r, tm), _tile_rows(sin_r, tm), tm)
    z, s_fin = _ret_scan(q, k, v, sg, state0, b, t)
    x1 = _mix_ffn(xf, z, w["w_ret_out"], w["g_ffn0"], w["w_ffn_in0"], w["w_ffn_out0"], w["g_final"],
                  False, min(TM_FFN, n))

    half = QK_ROPE // 2
    inv_freq = ROPE_THETA ** (-jnp.arange(half, dtype=F32) / half)
    npad = LANES - 2 * QK_ROPE
    inv_g = jnp.concatenate([jnp.tile(inv_freq, 4), jnp.zeros((npad,), F32)])
    sign_g = jnp.concatenate([jnp.tile(jnp.repeat(jnp.array([-1.0, 1.0], F32), half), 2),
                              jnp.zeros((npad,), F32)])
    ang_g = pos.astype(F32)[:, None] * inv_g[None, :]
    cos_g = jnp.cos(ang_g)
    sin_g = jnp.sin(ang_g) * sign_g[None, :]
    tm = min(TM_KVQ, n)
    outs = _kvq(x1, w, _tile_rows(cos_g, tm), _tile_rows(sin_g, tm), b, t, tm, is_prompt)
    ckv, kr = outs[:2]

    if is_prompt:
        qcat, kcat, vcat = outs[2:]
        o = _attn_heads(qcat, kcat, vcat, b, t)
    else:
        ckv_b, kr_b, qcat = outs[2:]
        o = _attn(qcat, past_ckv, past_krope, ckv_b, kr_b, w["wuk_pairs"], w["wuv_pairs"], b)

    y = _mix_ffn(x1, o, w["w_mla_out"], w["g_ffn1"], w["w_ffn_in1"], w["w_ffn_out1"], w["g_final"],
                 True, min(TM_FFN, n))
    return (y.reshape(b, t, D_MODEL), s_fin[None], ckv.reshape(b, t, KV_LORA), kr.reshape(b, t, QK_ROPE))


def kernel(x_prompt, x_sample, state_ret, cache_ckv, cache_krope, g_mix, g_ffn, w_ret_in, w_ret_out,
           g_kv_in, w_dkv, g_ckv, w_uk, w_uv, w_dq, g_q, w_uq, w_mla_out, w_ffn_in, w_ffn_out, g_final):
    assert g_mix.shape[0] == 2 and w_ret_in.shape[0] == 1 and w_dq.shape[0] == 1
    assert x_prompt.shape[1] % ATTN_TQ == 0 and x_sample.shape[1] == CHUNK
    w = _prep_weights(g_mix, g_ffn, w_ret_in, w_ret_out, g_kv_in, w_dkv, g_ckv, w_uk, w_uv,
                      w_dq, g_q, w_uq, w_mla_out, w_ffn_in, w_ffn_out, g_final)
    past = cache_ckv.shape[1]
    pos_p = jnp.arange(x_prompt.shape[1], dtype=jnp.int32)
    pos_s = past + jnp.arange(x_sample.shape[1], dtype=jnp.int32)
    zero_state = jnp.zeros((x_prompt.shape[0], RET_HEADS, RET_DK, RET_DV), F32)
    y_p, st_p, ckv_p, kr_p = _trunk(x_prompt, pos_p, zero_state, None, None, w)
    y_s, st_s, ckv_s, kr_s = _trunk(x_sample, pos_s, state_ret[0], cache_ckv, cache_krope, w)
    return (y_p, y_s, st_p.astype(state_ret.dtype), st_s.astype(state_ret.dtype), ckv_p, kr_p, ckv_s, kr_s)
```

```python
import functools
import math

import jax
import jax.numpy as jnp
from jax import lax
from jax.experimental import pallas as pl
from jax.experimental.pallas import tpu as pltpu

F32 = jnp.float32
BF16 = jnp.bfloat16

D_MODEL = 1024
CHUNK = 64
RET_HEADS = 4
RET_DK = D_MODEL // RET_HEADS
RET_DV = 2 * D_MODEL // RET_HEADS
RET_QK = RET_HEADS * RET_DK
RET_V = RET_HEADS * RET_DV
MLA_HEADS = 16
QK_NOPE = 64
QK_ROPE = 32
V_HEAD = 64
Q_LORA = 384
KV_LORA = 256
D_FF = 2816
ROPE_THETA = 10000.0
NORM_EPS = 1e-6
MLA_SCALE = (QK_NOPE + QK_ROPE) ** -0.5
NEG_INF = -1e30

V7X_VMEM_BYTES = 64 * 2**20
VMEM_LIMIT_BYTES = V7X_VMEM_BYTES - 8 * 2**20
LANES = 128
MXU_DIM = 256

RET_CHUNK = MXU_DIM
RET_HEAD_GROUP = 4
KV_BLOCK = MXU_DIM
ATTN_ROWS = 256
ATTN_TQ = 8 * KV_BLOCK
PAIR_W = 2 * LANES
LOG2E = math.log2(math.e)
TM_RET_IN = 512
TM_FFN = 512
TM_KVQ = 256

_NT = (((1,), (1,)), ((), ()))
_TN = (((0,), (0,)), ((), ()))


def _cparams(*sem):
    return pltpu.CompilerParams(dimension_semantics=sem, vmem_limit_bytes=VMEM_LIMIT_BYTES)


def _resident(shape):
    nd = len(shape)
    return pl.BlockSpec(shape, lambda *_: (0,) * nd, pipeline_mode=pl.Buffered(1))


def _rms(x, g):
    return x * lax.rsqrt(jnp.mean(x * x, axis=-1, keepdims=True) + NORM_EPS) * g


def _dot(a, b):
    return jnp.dot(a, b, preferred_element_type=F32)


def _silu(x):
    return x * jax.nn.sigmoid(x)


def _ret_in_kernel(x_ref, g_ref, w_ref, cos_ref, sin_ref, q_ref, k_ref, v_ref, sg_ref):
    hb = _rms(x_ref[...], g_ref[...]).astype(BF16)
    cos = cos_ref[...]
    sin = sin_ref[...]
    half = RET_DK // 2
    for dst, base, scale in ((q_ref, 0, 1.0), (k_ref, RET_QK, RET_DK ** -0.5)):
        for hd in range(RET_HEADS):
            c0 = hd * RET_DK
            y = _dot(hb, w_ref[:, base + c0:base + c0 + RET_DK])
            y1 = y[:, :half]
            y2 = y[:, half:]
            dst[:, c0:c0 + half] = ((y1 * cos - y2 * sin) * scale).astype(BF16)
            dst[:, c0 + half:c0 + RET_DK] = ((y2 * cos + y1 * sin) * scale).astype(BF16)
    v_ref[...] = _dot(hb, w_ref[:, 2 * RET_QK:2 * RET_QK + RET_V]).astype(BF16)
    sg_ref[...] = _silu(_dot(hb, w_ref[:, 2 * RET_QK + RET_V:])).astype(BF16)


def _ret_in(x, g, w, cos, sin, tm):
    n = x.shape[0]
    nt = cos.shape[0] // tm
    tok = lambda d: pl.BlockSpec((tm, d), lambda i: (i, 0))
    tab = pl.BlockSpec((tm, RET_DK // 2), lambda i: (i % nt, 0))
    return pl.pallas_call(
        _ret_in_kernel,
        grid=(n // tm,),
        in_specs=[tok(D_MODEL), _resident(g.shape), _resident(w.shape), tab, tab],
        out_specs=[tok(RET_QK), tok(RET_QK), tok(RET_V), tok(RET_V)],
        out_shape=[jax.ShapeDtypeStruct((n, RET_QK), BF16), jax.ShapeDtypeStruct((n, RET_QK), BF16),
                   jax.ShapeDtypeStruct((n, RET_V), BF16), jax.ShapeDtypeStruct((n, RET_V), BF16)],
        compiler_params=_cparams("parallel"),
        name="ret_in",
    )(x, g, w, cos, sin)


def _ret_scan_kernel(q_ref, k_ref, v_ref, sg_ref, dm_ref, qd_ref, kd_ref, cd_ref, s0_ref,
                     z_ref, sout_ref, s_scr):
    c = pl.program_id(2)

    @pl.when(c == 0)
    def _():
        s_scr[...] = s0_ref[...]

    for h in range(s_scr.shape[0]):
        qs = slice(h * RET_DK, (h + 1) * RET_DK)
        vs = slice(h * RET_DV, (h + 1) * RET_DV)
        q = q_ref[:, qs]
        k = k_ref[:, qs]
        v = v_ref[:, vs]
        s = s_scr[h]
        a = lax.dot_general(q, k, _NT, preferred_element_type=F32) * dm_ref[h]
        y = _dot(a.astype(BF16), v) + qd_ref[h] * _dot(q, s.astype(BF16))
        kd = (k.astype(F32) * kd_ref[h]).astype(BF16)
        s_scr[h] = s * cd_ref[h] + lax.dot_general(kd, v, _TN, preferred_element_type=F32)
        yn = y * lax.rsqrt(jnp.mean(y * y, axis=-1, keepdims=True) + NORM_EPS)
        z_ref[:, vs] = (sg_ref[:, vs].astype(F32) * yn).astype(BF16)

    @pl.when(c == pl.num_programs(2) - 1)
    def _():
        sout_ref[...] = s_scr[...]


def _ret_scan(q, k, v, sg, state0, b, t):
    ck = min(RET_CHUNK, t)
    nc = t // ck
    lg = jnp.log(1.0 - 2.0 ** (-5.0 - jnp.arange(RET_HEADS, dtype=F32)))
    idx = jnp.arange(ck, dtype=F32)
    diff = idx[:, None] - idx[None, :]
    dmask = jnp.where(diff >= 0, jnp.exp(jnp.maximum(diff, 0.0)[None] * lg[:, None, None]), 0.0)
    qdec = jnp.exp((idx + 1.0)[None, :] * lg[:, None])[:, :, None]
    kdec = jnp.exp((ck - 1.0 - idx)[None, :] * lg[:, None])[:, :, None]
    cdec = jnp.broadcast_to(jnp.exp(ck * lg)[:, None, None], (RET_HEADS, 1, RET_DV))

    hg = RET_HEAD_GROUP
    qk_spec = pl.BlockSpec((None, ck, hg * RET_DK), lambda bi, h, c: (bi, c, h))
    v_spec = pl.BlockSpec((None, ck, hg * RET_DV), lambda bi, h, c: (bi, c, h))
    head = lambda r, w: pl.BlockSpec((hg, r, w), lambda bi, h, c: (h, 0, 0))
    st_spec = pl.BlockSpec((None, hg, RET_DK, RET_DV), lambda bi, h, c: (bi, h, 0, 0))
    z, s_out = pl.pallas_call(
        _ret_scan_kernel,
        grid=(b, RET_HEADS // hg, nc),
        in_specs=[qk_spec, qk_spec, v_spec, v_spec, head(ck, ck), head(ck, 1), head(ck, 1),
                  head(1, RET_DV), st_spec],
        out_specs=[v_spec, st_spec],
        out_shape=[jax.ShapeDtypeStruct((b, t, RET_V), BF16),
                   jax.ShapeDtypeStruct((b, RET_HEADS, RET_DK, RET_DV), F32)],
        scratch_shapes=[pltpu.VMEM((hg, RET_DK, RET_DV), F32)],
        compiler_params=_cparams("parallel", "parallel", "arbitrary"),
        name="ret_scan",
    )(q.reshape(b, t, RET_QK), k.reshape(b, t, RET_QK), v.reshape(b, t, RET_V), sg.reshape(b, t, RET_V),
      dmask, qdec, kdec, cdec, state0)
    return z.reshape(b * t, RET_V), s_out


def _mix_ffn_kernel(x_ref, z_ref, wm_ref, g_ref, win_ref, wout_ref, gf_ref, o_ref, *, final):
    x1 = x_ref[...] + _dot(z_ref[...], wm_ref[...])
    hb = _rms(x1, g_ref[...]).astype(BF16)
    gu = _dot(hb, win_ref[...])
    a = (_silu(gu[:, :D_FF]) * gu[:, D_FF:]).astype(BF16)
    x2 = x1 + _dot(a, wout_ref[...])
    if final:
        x2 = _rms(x2, gf_ref[...])
    o_ref[...] = x2


def _mix_ffn(x, z, wm, g, win, wout, gf, final, tm):
    n = x.shape[0]
    tok = lambda d: pl.BlockSpec((tm, d), lambda i: (i, 0))
    return pl.pallas_call(
        functools.partial(_mix_ffn_kernel, final=final),
        grid=(n // tm,),
        in_specs=[tok(D_MODEL), tok(z.shape[1]), _resident(wm.shape), _resident(g.shape),
                  _resident(win.shape), _resident(wout.shape), _resident(gf.shape)],
        out_specs=tok(D_MODEL),
        out_shape=jax.ShapeDtypeStruct((n, D_MODEL), F32),
        compiler_params=_cparams("parallel"),
        name="mix_ffn_final" if final else "mix_ffn",
    )(x, z, wm, g, win, wout, gf)


def _kvq_kernel(x_ref, gkv_ref, wdkv_ref, gckv_ref, gmix_ref, wdq_ref, gq_ref, wq1_ref, wq2_ref,
                wuk_ref, wuv_ref, cos_ref, sin_ref,
                ckv_ref, kr_ref, *out_refs, expand):
    if expand:
        q_ref, k_ref, v_ref = out_refs
    else:
        ckvb_ref, krb_ref, q_ref = out_refs
    x = x_ref[...]
    xn = x * lax.rsqrt(jnp.mean(x * x, axis=-1, keepdims=True) + NORM_EPS)
    cos = cos_ref[...]
    sin = sin_ref[...]
    ckr = _dot((xn * gkv_ref[...]).astype(BF16), wdkv_ref[...])
    ckv = _rms(ckr[:, :KV_LORA], gckv_ref[...])
    ckvb = ckv.astype(BF16)
    ckv_ref[...] = ckv
    krot = ckr[:, KV_LORA:KV_LORA + LANES] * cos + ckr[:, KV_LORA + LANES:] * sin
    kr_ref[...] = krot[:, :QK_ROPE]
    if not expand:
        ckvb_ref[...] = ckvb
        krb_ref[...] = krot[:, :QK_ROPE].astype(BF16)
    qscale = MLA_SCALE * LOG2E
    cq = _rms(_dot((xn * gmix_ref[...]).astype(BF16), wdq_ref[...]), gq_ref[...]).astype(BF16)
    y1 = _dot(cq, wq1_ref[...]) * qscale
    y2 = _dot(cq, wq2_ref[...]) * qscale
    for p in range(MLA_HEADS // 2):
        lo = p * PAIR_W
        rot = y1[:, lo + LANES:lo + PAIR_W] * cos + y2[:, p * LANES:(p + 1) * LANES] * sin
        qg = jnp.concatenate([y1[:, lo:lo + LANES], rot], axis=1).astype(BF16)
        if expand:
            q_ref[p] = qg
        else:
            q_ref[:, lo:lo + PAIR_W] = qg
    if expand:
        kn = _dot(ckvb, wuk_ref[...])
        vv = _dot(ckvb, wuv_ref[...])
        krb = krot.astype(BF16)
        for p in range(MLA_HEADS // 2):
            k_ref[p] = jnp.concatenate([kn[:, p * LANES:(p + 1) * LANES].astype(BF16), krb], axis=1)
            v_ref[p] = vv[:, p * LANES:(p + 1) * LANES].astype(BF16)


def _kvq(x, w, cos, sin, b, t, tm, expand):
    n = b * t
    nt = cos.shape[0] // tm
    npairs = MLA_HEADS // 2
    tok = lambda d: pl.BlockSpec((tm, d), lambda i: (i, 0))
    tab = pl.BlockSpec((tm, LANES), lambda i: (i % nt, 0))
    res = [w["g_kv_in"], w["wdkv"], w["g_ckv"], w["g_mix1"], w["w_dq"], w["g_q"], w["wq1"], w["wq2"],
           w["wuk_all"], w["wuv_all"]]
    sds = lambda d, dt: jax.ShapeDtypeStruct((n, d), dt)
    out_specs = [tok(KV_LORA), tok(QK_ROPE)]
    out_shape = [sds(KV_LORA, F32), sds(QK_ROPE, F32)]
    if expand:
        tpb = t // tm
        pm = lambda d: pl.BlockSpec((None, npairs, tm, d), lambda i: (i // tpb, 0, i % tpb, 0))
        pm_sds = lambda d: jax.ShapeDtypeStruct((b, npairs, t, d), BF16)
        out_specs += [pm(PAIR_W), pm(PAIR_W), pm(LANES)]
        out_shape += [pm_sds(PAIR_W), pm_sds(PAIR_W), pm_sds(LANES)]
    else:
        out_specs += [tok(KV_LORA), tok(QK_ROPE), tok(npairs * PAIR_W)]
        out_shape += [sds(KV_LORA, BF16), sds(QK_ROPE, BF16), sds(npairs * PAIR_W, BF16)]
    return pl.pallas_call(
        functools.partial(_kvq_kernel, expand=expand),
        grid=(n // tm,),
        in_specs=[tok(D_MODEL)] + [_resident(a.shape) for a in res] + [tab, tab],
        out_specs=out_specs,
        out_shape=out_shape,
        compiler_params=_cparams("parallel"),
        name="kvq_expand" if expand else "kvq",
    )(x, *res, cos, sin)


def _attn_heads_kernel(q_ref, k_ref, v_ref, o_ref, q_s, m_s, l_s, acc_s, sa_s, sb_s):
    tq = q_ref.shape[0]
    kb = k_ref.shape[1]
    rows = 2 * tq
    i = pl.program_id(2)
    q = q_ref[...].astype(F32)
    grp = lax.broadcasted_iota(jnp.int32, q.shape, 1) // QK_ROPE
    zero = jnp.zeros_like(q)
    q_s[0:tq, :] = jnp.where(grp < 2, q, jnp.where(grp == 4, q, zero)).astype(BF16)
    q_s[tq:rows, :] = jnp.where(grp < 2, zero, jnp.where(grp < 4, q, jnp.where(grp == 5, q, zero))).astype(BF16)
    m_s[...] = jnp.full(m_s.shape, NEG_INF, F32)
    l_s[...] = jnp.zeros(l_s.shape, F32)
    acc_s[...] = jnp.zeros(acc_s.shape, F32)

    def scores(j, dst, t0=0):
        if t0 == 0:
            dst[...] = lax.dot_general(q_s[...], k_ref[j], _NT, preferred_element_type=F32)
        else:
            for h in range(2):
                rs = slice(h * tq + t0, (h + 1) * tq)
                dst[rs, :] = lax.dot_general(q_s[rs, :], k_ref[j], _NT, preferred_element_type=F32)

    def softmax_pv(j, src, diag):
        vblk = v_ref[j]
        for r in range(rows // ATTN_ROWS):
            rs = slice(r * ATTN_ROWS, (r + 1) * ATTN_ROWS)
            s = src[rs, :]
            if diag is not None:
                qc0 = ((r * ATTN_ROWS) % tq) // CHUNK
                qc1 = qc0 + ATTN_ROWS // CHUNK - 1
                kc0 = diag * (kb // CHUNK)
                kc1 = kc0 + kb // CHUNK - 1
                if kc0 > qc1:
                    continue
                if kc1 > qc0:
                    qc = qc0 + lax.broadcasted_iota(jnp.int32, s.shape, 0) // CHUNK
                    kc = kc0 + lax.broadcasted_iota(jnp.int32, s.shape, 1) // CHUNK
                    s = jnp.where(kc <= qc, s, NEG_INF)
            m_prev = m_s[rs, :]
            m_new = jnp.maximum(m_prev, jnp.max(s, axis=1, keepdims=True))
            alpha = jnp.exp2(m_prev - m_new)
            p = jnp.exp2(s - jnp.concatenate([m_new] * (kb // LANES), axis=1))
            psum = p[:, :LANES]
            for c in range(1, kb // LANES):
                psum = psum + p[:, c * LANES:(c + 1) * LANES]
            l_s[rs, :] = alpha * l_s[rs, :] + psum
            acc_s[rs, :] = alpha * acc_s[rs, :] + _dot(p.astype(BF16), vblk)
            m_s[rs, :] = m_new

    nb = tq // kb
    bufs = (sa_s, sb_s)
    scores(0, sa_s)

    def full_blocks(j, count):
        for d in range(count):
            scores(j + d + 1, bufs[(d + 1) % 2])
            softmax_pv(j + d, bufs[d % 2], None)

    def body(jj, carry):
        full_blocks(2 * nb * jj, 2 * nb)
        return carry

    lax.fori_loop(0, i // 2, body, 0)

    @pl.when(i % 2 == 1)
    def _():
        full_blocks(nb * (i - 1), nb)

    j0 = nb * i
    for d in range(nb):
        if d + 1 < nb:
            scores(j0 + d + 1, bufs[(d + 1) % 2], (d + 1) * kb)
        softmax_pv(j0 + d, bufs[d % 2], d)

    o = acc_s[...] / jnp.sum(l_s[...], axis=1, keepdims=True)
    lane = lax.broadcasted_iota(jnp.int32, (tq, LANES), 1)
    o_ref[...] = jnp.where(lane < V_HEAD, o[0:tq, :], o[tq:rows, :]).astype(BF16)


def _attn_heads(qcat, kcat, vcat, b, t):
    tq = ATTN_TQ
    assert tq % (2 * KV_BLOCK) == 0 and t % tq == 0
    nq = t // tq
    nkb = t // KV_BLOCK
    npairs = MLA_HEADS // 2
    rows = 2 * tq
    kv = lambda d: pl.BlockSpec((None, None, nkb, KV_BLOCK, d), lambda bi, p, i: (bi, p, 0, 0, 0))
    return pl.pallas_call(
        _attn_heads_kernel,
        grid=(b, npairs, nq),
        in_specs=[pl.BlockSpec((None, None, tq, PAIR_W), lambda bi, p, i: (bi, p, i, 0)),
                  kv(PAIR_W), kv(LANES)],
        out_specs=pl.BlockSpec((tq, LANES), lambda bi, p, i: (bi * nq + i, p)),
        out_shape=jax.ShapeDtypeStruct((b * t, MLA_HEADS * V_HEAD), BF16),
        scratch_shapes=[pltpu.VMEM((rows, PAIR_W), BF16),
                        pltpu.VMEM((rows, LANES), F32), pltpu.VMEM((rows, LANES), F32),
                        pltpu.VMEM((rows, LANES), F32),
                        pltpu.VMEM((rows, KV_BLOCK), F32), pltpu.VMEM((rows, KV_BLOCK), F32)],
        compiler_params=_cparams("parallel", "parallel", "arbitrary"),
        name="attn_heads",
    )(qcat, kcat.reshape(b, npairs, nkb, KV_BLOCK, PAIR_W), vcat.reshape(b, npairs, nkb, KV_BLOCK, LANES))


def _attn_kernel(q_ref, pc_ref, pr_ref, nc_ref, nr_ref, wuk_ref, wuv_ref, o_ref,
                 q_s, qr_s, m_s, l_s, acc_s, sa_s, sb_s):
    tq = q_ref.shape[0]
    nblk = pc_ref.shape[0] // KV_BLOCK
    past_rows = lambda j: pl.ds(pl.multiple_of(j * KV_BLOCK, KV_BLOCK), KV_BLOCK)
    rows = q_s.shape[0]
    for g in range(MLA_HEADS // 2):
        lo = g * PAIR_W
        ql = _dot(q_ref[:, lo:lo + LANES], wuk_ref[g])
        q_s[(2 * g) * tq:(2 * g + 1) * tq, :] = ql[:, :KV_LORA].astype(BF16)
        q_s[(2 * g + 1) * tq:(2 * g + 2) * tq, :] = ql[:, KV_LORA:].astype(BF16)
        for e in range(2):
            qr_s[(2 * g + e) * tq:(2 * g + e + 1) * tq, :] = (
                q_ref[:, lo + LANES + e * QK_ROPE:lo + LANES + (e + 1) * QK_ROPE])
    m_s[...] = jnp.full(m_s.shape, NEG_INF, F32)
    l_s[...] = jnp.zeros(l_s.shape, F32)
    acc_s[...] = jnp.zeros(acc_s.shape, F32)

    def scores(j, dst):
        dst[...] = (lax.dot_general(q_s[...], pc_ref[past_rows(j), :].astype(BF16), _NT, preferred_element_type=F32)
                    + lax.dot_general(qr_s[...], pr_ref[past_rows(j), :].astype(BF16), _NT, preferred_element_type=F32))

    def softmax_pv(get_s, vblk):
        nk = vblk.shape[0]
        for r in range(rows // ATTN_ROWS):
            rs = slice(r * ATTN_ROWS, (r + 1) * ATTN_ROWS)
            s = get_s(rs)
            m_prev = m_s[rs, :]
            m_new = jnp.maximum(m_prev, jnp.max(s, axis=1, keepdims=True))
            alpha = jnp.exp2(m_prev - m_new)
            if nk >= LANES:
                p = jnp.exp2(s - jnp.concatenate([m_new] * (nk // LANES), axis=1))
                psum = p[:, :LANES]
                for c in range(1, nk // LANES):
                    psum = psum + p[:, c * LANES:(c + 1) * LANES]
            else:
                p = jnp.exp2(s - m_new[:, :nk])
                psum = jnp.concatenate([p, jnp.zeros((ATTN_ROWS, LANES - nk), F32)], axis=1)
            l_s[rs, :] = alpha * l_s[rs, :] + psum
            acc_s[rs, :] = (jnp.concatenate([alpha] * (KV_LORA // LANES), axis=1) * acc_s[rs, :]
                            + _dot(p.astype(BF16), vblk))
            m_s[rs, :] = m_new

    def past_block(j, src):
        softmax_pv(lambda rs: src[rs, :], pc_ref[past_rows(j), :].astype(BF16))

    bufs = (sa_s, sb_s)
    per_iter = 4
    niter = (nblk - 1) // per_iter
    scores(0, sa_s)

    def body(jj, carry):
        j = per_iter * jj
        for d in range(per_iter):
            scores(j + d + 1, bufs[(d + 1) % 2])
            past_block(j + d, bufs[d % 2])
        return carry

    lax.fori_loop(0, niter, body, 0)
    j0 = per_iter * niter
    for d in range(nblk - j0):
        if j0 + d + 1 < nblk:
            scores(j0 + d + 1, bufs[(d + 1) % 2])
        past_block(j0 + d, bufs[d % 2])

    nc = nc_ref[...]
    nr = nr_ref[...]
    softmax_pv(lambda rs: (lax.dot_general(q_s[rs, :], nc, _NT, preferred_element_type=F32)
                           + lax.dot_general(qr_s[rs, :], nr, _NT, preferred_element_type=F32)), nc)

    o_lat = acc_s[...] / jnp.sum(l_s[...], axis=1, keepdims=True)
    for g in range(MLA_HEADS // 2):
        pair = jnp.concatenate([o_lat[(2 * g) * tq:(2 * g + 1) * tq],
                                o_lat[(2 * g + 1) * tq:(2 * g + 2) * tq]], axis=1).astype(BF16)
        o_ref[:, g * LANES:(g + 1) * LANES] = _dot(pair, wuv_ref[g]).astype(BF16)


def _attn(qcat, past_ckv, past_krope, new_ckv, new_krope, wuk_pairs, wuv_pairs, b):
    tq = CHUNK
    past = past_ckv.shape[1]
    assert past % KV_BLOCK == 0 and past >= KV_BLOCK
    rows = MLA_HEADS * tq
    qspec = lambda d: pl.BlockSpec((tq, d), lambda bi: (bi, 0))
    past_spec = lambda d: pl.BlockSpec((None, past, d), lambda bi: (bi, 0, 0))
    return pl.pallas_call(
        _attn_kernel,
        grid=(b,),
        in_specs=[qspec(qcat.shape[1]), past_spec(KV_LORA), past_spec(QK_ROPE),
                  qspec(KV_LORA), qspec(QK_ROPE),
                  _resident(wuk_pairs.shape), _resident(wuv_pairs.shape)],
        out_specs=qspec(MLA_HEADS * V_HEAD),
        out_shape=jax.ShapeDtypeStruct((b * tq, MLA_HEADS * V_HEAD), BF16),
        scratch_shapes=[pltpu.VMEM((rows, KV_LORA), BF16), pltpu.VMEM((rows, QK_ROPE), BF16),
                        pltpu.VMEM((rows, LANES), F32), pltpu.VMEM((rows, LANES), F32),
                        pltpu.VMEM((rows, KV_LORA), F32),
                        pltpu.VMEM((rows, KV_BLOCK), F32), pltpu.VMEM((rows, KV_BLOCK), F32)],
        compiler_params=_cparams("parallel"),
        name="attn_full",
    )(qcat, past_ckv, past_krope, new_ckv, new_krope, wuk_pairs, wuv_pairs)


def _rope_tables(pos, half):
    inv_freq = ROPE_THETA ** (-jnp.arange(half, dtype=F32) / half)
    ang = pos.astype(F32)[:, None] * inv_freq[None, :]
    return jnp.cos(ang), jnp.sin(ang)


def _tile_rows(tab, tm):
    reps = -(-tm // tab.shape[0])
    return jnp.tile(tab, (reps, 1)) if reps > 1 else tab


def _prep_weights(g_mix, g_ffn, w_ret_in, w_ret_out, g_kv_in, w_dkv, g_ckv, w_uk, w_uv,
                  w_dq, g_q, w_uq, w_mla_out, w_ffn_in, w_ffn_out, g_final):
    half = QK_ROPE // 2
    row = lambda g: g.reshape(1, -1)
    npairs = MLA_HEADS // 2
    swap = lambda a: jnp.concatenate([a[..., half:], a[..., :half]], axis=-1)
    w_kr = w_dkv[:, KV_LORA:]
    zk64 = jnp.zeros((D_MODEL, LANES - 2 * QK_ROPE), F32)
    wdkv = jnp.concatenate([w_dkv[:, :KV_LORA], w_kr, w_kr, zk64, swap(w_kr), swap(w_kr), zk64],
                           axis=1).astype(BF16)
    wuq4 = w_uq[0].reshape(Q_LORA, npairs, 2, QK_NOPE + QK_ROPE)
    nope = wuq4[..., :QK_NOPE].reshape(Q_LORA, npairs, 2 * QK_NOPE)
    rope = wuq4[..., QK_NOPE:]
    zq64 = jnp.zeros((Q_LORA, npairs, LANES - 2 * QK_ROPE), F32)
    wq1 = jnp.concatenate([nope, rope.reshape(Q_LORA, npairs, 2 * QK_ROPE), zq64],
                          axis=2).reshape(Q_LORA, npairs * PAIR_W).astype(BF16)
    wq2 = jnp.concatenate([swap(rope).reshape(Q_LORA, npairs, 2 * QK_ROPE), zq64],
                          axis=2).reshape(Q_LORA, npairs * LANES).astype(BF16)
    wuk_t = jnp.transpose(w_uk, (1, 2, 0)).reshape(MLA_HEADS // 2, 2, QK_NOPE, KV_LORA)
    zk = jnp.zeros_like(wuk_t[:, 0])
    wuk_pairs = jnp.concatenate([jnp.concatenate([wuk_t[:, 0], zk], axis=2),
                                 jnp.concatenate([zk, wuk_t[:, 1]], axis=2)], axis=1).astype(BF16)
    wuv_h = jnp.transpose(w_uv, (1, 0, 2)).reshape(MLA_HEADS // 2, 2, KV_LORA, V_HEAD)
    zv = jnp.zeros_like(wuv_h[:, 0])
    wuv_pairs = jnp.concatenate([jnp.concatenate([wuv_h[:, 0], zv], axis=2),
                                 jnp.concatenate([zv, wuv_h[:, 1]], axis=2)], axis=1).astype(BF16)
    return dict(
        g_mix0=row(g_mix[0]), g_mix1=row(g_mix[1]), g_ffn0=row(g_ffn[0]), g_ffn1=row(g_ffn[1]),
        w_ret_in=w_ret_in[0].astype(BF16), w_ret_out=w_ret_out[0].astype(BF16),
        g_kv_in=row(g_kv_in), wdkv=wdkv, g_ckv=row(g_ckv), wuk_pairs=wuk_pairs, wuv_pairs=wuv_pairs,
        w_dq=w_dq[0].astype(BF16), g_q=row(g_q[0]), wq1=wq1, wq2=wq2,
        wuk_all=w_uk.reshape(KV_LORA, -1).astype(BF16), wuv_all=w_uv.reshape(KV_LORA, -1).astype(BF16),
        w_mla_out=w_mla_out[0].astype(BF16),
        w_ffn_in0=w_ffn_in[0].astype(BF16), w_ffn_in1=w_ffn_in[1].astype(BF16),
        w_ffn_out0=w_ffn_out[0].astype(BF16), w_ffn_out1=w_ffn_out[1].astype(BF16),
        g_final=row(g_final))


def _trunk(x, pos, state0, past_ckv, past_krope, w):
    b, t, _ = x.shape
    n = b * t
    is_prompt = past_ckv is None
    xf = x.reshape(n, D_MODEL)

    cos_r, sin_r = _rope_tables(pos, RET_DK // 2)
    tm = min(TM_RET_IN, n)
    q, k, v, sg = _ret_in(xf, w["g_mix0"], w["w_ret_in"], _tile_rows(cos_r, tm), _tile_rows(sin_r, tm), tm)
    z, s_fin = _ret_scan(q, k, v, sg, state0, b, t)
    x1 = _mix_ffn(xf, z, w["w_ret_out"], w["g_ffn0"], w["w_ffn_in0"], w["w_ffn_out0"], w["g_final"],
                  False, min(TM_FFN, n))

    half = QK_ROPE // 2
    inv_freq = ROPE_THETA ** (-jnp.arange(half, dtype=F32) / half)
    npad = LANES - 2 * QK_ROPE
    inv_g = jnp.concatenate([jnp.tile(inv_freq, 4), jnp.zeros((npad,), F32)])
    sign_g = jnp.concatenate([jnp.tile(jnp.repeat(jnp.array([-1.0, 1.0], F32), half), 2),
                              jnp.zeros((npad,), F32)])
    ang_g = pos.astype(F32)[:, None] * inv_g[None, :]
    cos_g = jnp.cos(ang_g)
    sin_g = jnp.sin(ang_g) * sign_g[None, :]
    tm = min(TM_KVQ, n)
    outs = _kvq(x1, w, _tile_rows(cos_g, tm), _tile_rows(sin_g, tm), b, t, tm, is_prompt)
    ckv, kr = outs[:2]

    if is_prompt:
        qcat, kcat, vcat = outs[2:]
        o = _attn_heads(qcat, kcat, vcat, b, t)
    else:
        ckv_b, kr_b, qcat = outs[2:]
        o = _attn(qcat, past_ckv, past_krope, ckv_b, kr_b, w["wuk_pairs"], w["wuv_pairs"], b)

    y = _mix_ffn(x1, o, w["w_mla_out"], w["g_ffn1"], w["w_ffn_in1"], w["w_ffn_out1"], w["g_final"],
                 True, min(TM_FFN, n))
    return (y.reshape(b, t, D_MODEL), s_fin[None], ckv.reshape(b, t, KV_LORA), kr.reshape(b, t, QK_ROPE))


def kernel(x_prompt, x_sample, state_ret, cache_ckv, cache_krope, g_mix, g_ffn, w_ret_in, w_ret_out,
           g_kv_in, w_dkv, g_ckv, w_uk, w_uv, w_dq, g_q, w_uq, w_mla_out, w_ffn_in, w_ffn_out, g_final):
    assert g_mix.shape[0] == 2 and w_ret_in.shape[0] == 1 and w_dq.shape[0] == 1
    assert x_prompt.shape[1] % ATTN_TQ == 0 and x_sample.shape[1] == CHUNK
    w = _prep_weights(g_mix, g_ffn, w_ret_in, w_ret_out, g_kv_in, w_dkv, g_ckv, w_uk, w_uv,
                      w_dq, g_q, w_uq, w_mla_out, w_ffn_in, w_ffn_out, g_final)
    past = cache_ckv.shape[1]
    pos_p = jnp.arange(x_prompt.shape[1], dtype=jnp.int32)
    pos_s = past + jnp.arange(x_sample.shape[1], dtype=jnp.int32)
    zero_state = jnp.zeros((x_prompt.shape[0], RET_HEADS, RET_DK, RET_DV), F32)
    y_p, st_p, ckv_p, kr_p = _trunk(x_prompt, pos_p, zero_state, None, None, w)
    y_s, st_s, ckv_s, kr_s = _trunk(x_sample, pos_s, state_ret[0], cache_ckv, cache_krope, w)
    return (y_p, y_s, st_p.astype(state_ret.dtype), st_s.astype(state_ret.dtype), ckv_p, kr_p, ckv_s, kr_s)
```
